```python
import jax, jax.numpy as jnp
from jax import lax
import numpy as np

D_MODEL = 2048
BATCH = 16
SEQ = 2048
DEPTH = 1

CHUNK = 64
N_MEM = 256
FOX_HEADS = 16
FOX_HEAD_DIM = 128
FOX_WIDTH = FOX_HEADS * FOX_HEAD_DIM
CONV_WIDTH = 2048
CONV_TAPS = 3
MEM_HEADS = 4
MEM_HEAD_DIM = 128
MEM_WIDTH = MEM_HEADS * MEM_HEAD_DIM
D_FF = 5632
Q_BLOCK = 128
N_BRANCH = 2
EPS = 1e-6
FORGET_BIAS_INIT = 3.0
W_IN_COLS = 3 * FOX_WIDTH + FOX_HEADS + 3 * CONV_WIDTH + N_BRANCH * D_MODEL

kernel_name = "hybrid_fox_shortconv_macaron_sandwich"


def rmsnorm(x, g):
    xf = x.astype(jnp.float32)
    xf = xf * lax.rsqrt(jnp.mean(xf * xf, axis=-1, keepdims=True) + EPS)
    return (xf * g.astype(jnp.float32)).astype(x.dtype)


def swiglu(u, w13, w2):
    gate, up = jnp.split(u @ w13, 2, axis=-1)
    return (jax.nn.silu(gate) * up) @ w2


def forgetting_attention(q, k, v, log_f):
    b, s, h, dh = q.shape
    nb = s // Q_BLOCK
    scale = dh ** -0.5
    c = jnp.cumsum(log_f, axis=1).transpose(0, 2, 1)
    kh = k.transpose(0, 2, 1, 3)
    vh = v.transpose(0, 2, 1, 3)
    q_blocks = q.transpose(0, 2, 1, 3).reshape(b, h, nb, Q_BLOCK, dh).transpose(2, 0, 1, 3, 4)
    cq_blocks = c.reshape(b, h, nb, Q_BLOCK).transpose(2, 0, 1, 3)
    kpos = jnp.arange(s)

    def one_block(args):
        qb, cqb, i = args
        qpos = i * Q_BLOCK + jnp.arange(Q_BLOCK)
        logits = jnp.einsum('bhqd,bhkd->bhqk', qb, kh,
                            preferred_element_type=jnp.float32) * scale
        logits = logits + (cqb[..., :, None] - c[:, :, None, :])
        logits = jnp.where(kpos[None, :] <= qpos[:, None], logits, -jnp.inf)
        p = jax.nn.softmax(logits, axis=-1)
        return jnp.einsum('bhqk,bhkd->bhqd', p.astype(vh.dtype), vh)

    out = lax.map(one_block, (q_blocks, cq_blocks, jnp.arange(nb)))
    return out.transpose(1, 0, 3, 2, 4).reshape(b, s, h * dh)


def short_conv(z, w, bias):
    s = z.shape[1]
    zp = jnp.pad(z, ((0, 0), (CONV_TAPS - 1, 0), (0, 0)))
    y = bias
    for tap in range(CONV_TAPS):
        y = y + w[tap] * zp[:, tap:tap + s]
    return y


def memory_attention(u, mem_n, w_mq, w_mkv, w_mo):
    b, s, _ = u.shape
    m = mem_n.shape[1]
    q = (u @ w_mq).reshape(b, s, MEM_HEADS, MEM_HEAD_DIM)
    k, v = jnp.split(mem_n @ w_mkv, 2, axis=-1)
    k = k.reshape(b, m, MEM_HEADS, MEM_HEAD_DIM)
    v = v.reshape(b, m, MEM_HEADS, MEM_HEAD_DIM)
    logits = jnp.einsum('bshd,bmhd->bhsm', q, k,
                        preferred_element_type=jnp.float32) * (MEM_HEAD_DIM ** -0.5)
    p = jax.nn.softmax(logits, axis=-1)
    o = jnp.einsum('bhsm,bmhd->bshd', p.astype(v.dtype), v).reshape(b, s, MEM_WIDTH)
    return o @ w_mo


def setup_inputs(seed: int = 0) -> dict:
    key = jax.random.key(seed)
    ks = jax.random.split(key, 32)
    f32 = jnp.float32

    def w(k, shape, fan_in):
        return jax.random.normal(k, shape, f32) * (fan_in ** -0.5)

    def gain(k, shape):
        return 1.0 + 0.05 * jax.random.normal(k, shape, f32)

    L, D = DEPTH, D_MODEL
    return {
        "x": jax.random.normal(ks[0], (BATCH, SEQ, D), f32),
        "mem": jax.random.normal(ks[1], (BATCH, N_MEM, D), f32),
        "ffn1_pre_g": gain(ks[2], (L, D)),
        "ffn1_w13": w(ks[3], (L, D, 2 * D_FF), D),
        "ffn1_w2": w(ks[4], (L, D_FF, D), D_FF),
        "ffn1_post_g": gain(ks[5], (L, D)),
        "mix_pre_g": gain(ks[6], (L, D)),
        "w_in": w(ks[7], (L, D, W_IN_COLS), D),
        "forget_bias": FORGET_BIAS_INIT + 0.1 * jax.random.normal(ks[8], (L, FOX_HEADS), f32),
        "gate_bias": 0.02 * jax.random.normal(ks[9], (L, N_BRANCH, D), f32),
        "conv_w": w(ks[10], (L, CONV_TAPS, CONV_WIDTH), CONV_TAPS),
        "conv_b": 0.02 * jax.random.normal(ks[11], (L, CONV_WIDTH), f32),
        "w_out": w(ks[12], (L, D, D), D),
        "mix_post_g": gain(ks[13], (L, D)),
        "mem_q_pre_g": gain(ks[14], (L, D)),
        "mem_kv_g": gain(ks[15], (L, D)),
        "w_mq": w(ks[16], (L, D, MEM_WIDTH), D),
        "w_mkv": w(ks[17], (L, D, 2 * MEM_WIDTH), D),
        "w_mo": w(ks[18], (L, MEM_WIDTH, D), MEM_WIDTH),
        "mem_post_g": gain(ks[19], (L, D)),
        "ffn2_pre_g": gain(ks[20], (L, D)),
        "ffn2_w13": w(ks[21], (L, D, 2 * D_FF), D),
        "ffn2_w2": w(ks[22], (L, D_FF, D), D_FF),
        "ffn2_post_g": gain(ks[23], (L, D)),
    }


def reference(x, mem, ffn1_pre_g, ffn1_w13, ffn1_w2, ffn1_post_g, mix_pre_g, w_in,
              forget_bias, gate_bias, conv_w, conv_b, w_out, mix_post_g, mem_q_pre_g,
              mem_kv_g, w_mq, w_mkv, w_mo, mem_post_g, ffn2_pre_g, ffn2_w13, ffn2_w2,
              ffn2_post_g):
    b, s, _ = x.shape
    split_at = list(np.cumsum([FOX_WIDTH, FOX_WIDTH, FOX_WIDTH, FOX_HEADS,
                               CONV_WIDTH, CONV_WIDTH, CONV_WIDTH, D_MODEL]))
    h = x
    for l in range(DEPTH):
        f = swiglu(rmsnorm(h, ffn1_pre_g[l]), ffn1_w13[l], ffn1_w2[l])
        h = h + 0.5 * rmsnorm(f, ffn1_post_g[l])

        u = rmsnorm(h, mix_pre_g[l])
        p = u @ w_in[l]
        q, k, v, f_logit, cx, cb, cc, ga, gb = jnp.split(p, split_at, axis=-1)
        log_f = jax.nn.log_sigmoid((f_logit + forget_bias[l]).astype(jnp.float32))
        y_attn = forgetting_attention(q.reshape(b, s, FOX_HEADS, FOX_HEAD_DIM),
                                      k.reshape(b, s, FOX_HEADS, FOX_HEAD_DIM),
                                      v.reshape(b, s, FOX_HEADS, FOX_HEAD_DIM), log_f)
        y_conv = cb * short_conv(cc * cx, conv_w[l], conv_b[l])
        merged = (jax.nn.sigmoid(ga + gate_bias[l, 0]) * y_attn
                  + jax.nn.sigmoid(gb + gate_bias[l, 1]) * y_conv)
        h = h + rmsnorm(merged @ w_out[l], mix_post_g[l])

        m_out = memory_attention(rmsnorm(h, mem_q_pre_g[l]), rmsnorm(mem, mem_kv_g[l]),
                                 w_mq[l], w_mkv[l], w_mo[l])
        h = h + rmsnorm(m_out, mem_post_g[l])

        f = swiglu(rmsnorm(h, ffn2_pre_g[l]), ffn2_w13[l], ffn2_w2[l])
        h = h + 0.5 * rmsnorm(f, ffn2_post_g[l])
    return h
```

```python
import functools

import jax
import jax.numpy as jnp
from jax import lax
from jax.experimental import pallas as pl
from jax.experimental.pallas import tpu as pltpu

EPS = 1e-6
FOX_HEAD_DIM = 128
MEM_HEAD_DIM = 128
CONV_TAPS = 3

LANES = 128
SUBLANES = 8
VMEM_LIMIT_BYTES = 56 * 1024 * 1024
AUG_PER_HEAD = 8
NEG_BIG = -1e30

BF16 = jnp.bfloat16
F32 = jnp.float32


def _rms(x, g):
    ms = jnp.mean(x * x, axis=-1, keepdims=True)
    return x * lax.rsqrt(ms + EPS) * g


def _params(*sem):
    return pltpu.CompilerParams(dimension_semantics=sem, vmem_limit_bytes=VMEM_LIMIT_BYTES)


def _ffn_kernel(x_ref, gpre_ref, w1_ref, w3_ref, w2_ref, gpost_ref, o_ref, xn_ref):
    j = pl.program_id(1)

    @pl.when(j == 0)
    def _():
        xn_ref[...] = _rms(x_ref[...], gpre_ref[...]).astype(BF16)

    xn = xn_ref[...]
    gate = jnp.dot(xn, w1_ref[...], preferred_element_type=F32)
    up = jnp.dot(xn, w3_ref[...], preferred_element_type=F32)
    act = (gate * jax.nn.sigmoid(gate) * up).astype(BF16)
    contrib = jnp.dot(act, w2_ref[...], preferred_element_type=F32)

    @pl.when(j == 0)
    def _():
        o_ref[...] = contrib

    @pl.when(j > 0)
    def _():
        o_ref[...] += contrib

    @pl.when(j == pl.num_programs(1) - 1)
    def _():
        o_ref[...] = x_ref[...] + 0.5 * _rms(o_ref[...], gpost_ref[...])


def _ffn(x, g_pre, w13, w2, g_post, *, tm, tf):
    n, d = x.shape
    d_ff = w2.shape[0]
    nf = d_ff // tf
    return pl.pallas_call(
        _ffn_kernel,
        grid=(n // tm, nf),
        in_specs=[
            pl.BlockSpec((tm, d), lambda i, j: (i, 0)),
            pl.BlockSpec((1, d), lambda i, j: (0, 0)),
            pl.BlockSpec((d, tf), lambda i, j: (0, j)),
            pl.BlockSpec((d, tf), lambda i, j: (0, j + nf)),
            pl.BlockSpec((tf, d), lambda i, j: (j, 0)),
            pl.BlockSpec((1, d), lambda i, j: (0, 0)),
        ],
        out_specs=pl.BlockSpec((tm, d), lambda i, j: (i, 0)),
        out_shape=jax.ShapeDtypeStruct((n, d), F32),
        scratch_shapes=[pltpu.VMEM((tm, d), BF16)],
        compiler_params=_params("parallel", "arbitrary"),
        name="ffn",
    )(x, g_pre, w13, w13, w2, g_post)


def _proj_kernel(x_ref, g_ref, w_ref, o_ref, xn_ref, *, n_scaled, scale):
    j = pl.program_id(1)

    @pl.when(j == 0)
    def _():
        xn_ref[...] = _rms(x_ref[...], g_ref[...]).astype(BF16)

    acc = jnp.dot(xn_ref[...], w_ref[...], preferred_element_type=F32)
    if n_scaled:
        acc = acc * jnp.where(j < n_scaled, scale, 1.0).astype(F32)
    o_ref[...] = acc.astype(o_ref.dtype)


def _proj_f_kernel(x_ref, g_ref, w_ref, wf_ref, o_ref, f_ref, xn_ref, *, n_scaled, scale):
    j = pl.program_id(1)

    @pl.when(j == 0)
    def _():
        xn = _rms(x_ref[...], g_ref[...]).astype(BF16)
        xn_ref[...] = xn
        f_ref[...] = jnp.dot(xn, wf_ref[...], preferred_element_type=F32)

    acc = jnp.dot(xn_ref[...], w_ref[...], preferred_element_type=F32)
    if n_scaled:
        acc = acc * jnp.where(j < n_scaled, scale, 1.0).astype(F32)
    o_ref[...] = acc.astype(o_ref.dtype)


def _norm_proj(x, g, w, *, tm, tn, w_f=None, n_scaled=0, scale=1.0, name):
    n, d = x.shape
    cols = w.shape[1]
    in_specs = [
        pl.BlockSpec((tm, d), lambda i, j: (i, 0)),
        pl.BlockSpec((1, d), lambda i, j: (0, 0)),
        pl.BlockSpec((d, tn), lambda i, j: (0, j)),
    ]
    out_specs = pl.BlockSpec((tm, tn), lambda i, j: (i, j))
    out_shape = jax.ShapeDtypeStruct((n, cols), BF16)
    args = [x, g, w]
    if w_f is None:
        body = functools.partial(_proj_kernel, n_scaled=n_scaled, scale=scale)
    else:
        body = functools.partial(_proj_f_kernel, n_scaled=n_scaled, scale=scale)
        in_specs.append(pl.BlockSpec((d, LANES), lambda i, j: (0, 0)))
        out_specs = [out_specs, pl.BlockSpec((tm, LANES), lambda i, j: (i, 0))]
        out_shape = [out_shape, jax.ShapeDtypeStruct((n, LANES), F32)]
        args.append(w_f)
    return pl.pallas_call(
        body,
        grid=(n // tm, cols // tn),
        in_specs=in_specs,
        out_specs=out_specs,
        out_shape=out_shape,
        scratch_shapes=[pltpu.VMEM((tm, d), BF16)],
        compiler_params=_params("parallel", "arbitrary"),
        name=name,
    )(*args)


def _split3(x):
    hi = x.astype(BF16)
    r = x - hi.astype(F32)
    mid = r.astype(BF16)
    lo = (r - mid.astype(F32)).astype(BF16)
    return hi, mid, lo


def _forget_kernel(f_ref, b_ref, qa_ref, ka_ref, *, blk, n_heads):
    s = f_ref.shape[0]
    x = f_ref[...] + b_ref[...]
    logf = jnp.minimum(x, 0.0) - jnp.log1p(jnp.exp(-jnp.abs(x)))

    r = lax.broadcasted_iota(jnp.int32, (blk, blk), 0)
    c = lax.broadcasted_iota(jnp.int32, (blk, blk), 1)
    tri = (r >= c).astype(BF16)

    er = lax.broadcasted_iota(jnp.int32, (LANES, LANES), 0)
    ec = lax.broadcasted_iota(jnp.int32, (LANES, LANES), 1)
    lane = lax.broadcasted_iota(jnp.int32, (1, LANES), 1)
    slot = lane % AUG_PER_HEAD
    live = lane < n_heads * AUG_PER_HEAD

    def place(k):
        return ((ec == AUG_PER_HEAD * er + k) & (er < n_heads)).astype(BF16)

    q_ones = ((slot >= 3) & (slot < 6) & live).astype(F32)
    k_ones = ((slot < 3) & live).astype(F32)

    carry = jnp.zeros((1, LANES), F32)
    for b in range(s // blk):
        parts = _split3(logf[b * blk:(b + 1) * blk])
        local = sum(jnp.dot(tri, p, preferred_element_type=F32) for p in parts)
        cum = local + carry
        carry = cum[blk - 1:blk, :]
        c3 = _split3(cum)
        qa = sum(jnp.dot(c3[k], place(k), preferred_element_type=F32) for k in range(3))
        ka = sum(jnp.dot(c3[k], place(k + 3), preferred_element_type=F32) for k in range(3))
        qa_ref[b * blk:(b + 1) * blk, :] = (qa + q_ones).astype(BF16)
        ka_ref[b * blk:(b + 1) * blk, :] = (k_ones - ka).astype(BF16)


def _forget_aug(f, bias_row, *, batch, seq, n_heads):
    body = functools.partial(_forget_kernel, blk=256, n_heads=n_heads)
    spec = pl.BlockSpec((seq, LANES), lambda b: (b, 0))
    shape = jax.ShapeDtypeStruct((batch * seq, LANES), BF16)
    return pl.pallas_call(
        body,
        grid=(batch,),
        in_specs=[spec, pl.BlockSpec((1, LANES), lambda b: (0, 0))],
        out_specs=[spec, spec],
        out_shape=[shape, shape],
        compiler_params=_params("parallel"),
        name="forget_aug",
    )(f, bias_row)


def _attn_kernel(q_ref, k_ref, v_ref, qa_ref, ka_ref, o_ref, qc_ref, kc_ref, *, tq):
    s, dh = q_ref.shape
    h = pl.program_id(1)
    lane = lax.broadcasted_iota(jnp.int32, (1, LANES), 1)
    mine = (lane // AUG_PER_HEAD) == h
    qc_ref[:, :dh] = q_ref[...]
    qc_ref[:, dh:] = qa_ref[...]
    kc_ref[:, :dh] = k_ref[...]
    kc_ref[:, dh:] = jnp.where(mine, ka_ref[...], jnp.zeros_like(ka_ref[...]))

    row = lax.broadcasted_iota(jnp.int32, (tq, tq), 0)
    col = lax.broadcasted_iota(jnp.int32, (tq, tq), 1)
    causal = col <= row
    nt = (((1,), (1,)), ((), ()))

    for i in range(s // tq):
        lo = i * tq
        qi = qc_ref[lo:lo + tq, :]
        sd = lax.dot_general(qi, kc_ref[lo:lo + tq, :], nt, preferred_element_type=F32)
        sd = jnp.where(causal, sd, NEG_BIG)
        m = jnp.max(sd, axis=-1, keepdims=True)
        if i > 0:
            so = lax.dot_general(qi, kc_ref[0:lo, :], nt, preferred_element_type=F32)
            m = jnp.maximum(m, jnp.max(so, axis=-1, keepdims=True))
        pd = jnp.exp(sd - m)
        l = jnp.sum(pd, axis=-1, keepdims=True)
        acc = jnp.dot(pd.astype(BF16), v_ref[lo:lo + tq, :], preferred_element_type=F32)
        if i > 0:
            po = jnp.exp(so - m)
            l = l + jnp.sum(po, axis=-1, keepdims=True)
            acc = acc + jnp.dot(po.astype(BF16), v_ref[0:lo, :], preferred_element_type=F32)
        o_ref[lo:lo + tq, :] = (acc / l).astype(o_ref.dtype)


def _fox_attention(p, qa, ka, *, batch, seq, n_heads, tq):
    dh = FOX_HEAD_DIM
    n = batch * seq
    blk = lambda off: pl.BlockSpec((seq, dh), lambda b, h: (b, off + h))
    aug = pl.BlockSpec((seq, LANES), lambda b, h: (b, 0))
    return pl.pallas_call(
        functools.partial(_attn_kernel, tq=tq),
        grid=(batch, n_heads),
        in_specs=[blk(0), blk(n_heads), blk(2 * n_heads), aug, aug],
        out_specs=pl.BlockSpec((seq, dh), lambda b, h: (b, h)),
        out_shape=jax.ShapeDtypeStruct((n, n_heads * dh), BF16),
        scratch_shapes=[pltpu.VMEM((seq, dh + LANES), BF16),
                        pltpu.VMEM((seq, dh + LANES), BF16)],
        compiler_params=_params("parallel", "arbitrary"),
        name="fox_attn",
    )(p, p, p, qa, ka)


def _mix_out_kernel(h_ref, y_ref, cx_ref, cb_ref, cc_ref, ga_ref, gb_ref, cxh_ref, cch_ref,
                    gbias_ref, cw_ref, cbias_ref, wout_ref, gmix_ref,
                    gq_ref, wmq_ref, km_ref, vm_ref, wmo_ref, gmem_ref,
                    o_ref, *, tiles_per_seq, n_mem_heads):
    i = pl.program_id(0)
    tm = h_ref.shape[0]

    z = cc_ref[...].astype(F32) * cx_ref[...].astype(F32)
    zh = cch_ref[...].astype(F32) * cxh_ref[...].astype(F32)
    zh = zh * jnp.where(i % tiles_per_seq == 0, 0.0, 1.0).astype(F32)
    top = lax.broadcasted_iota(jnp.int32, (SUBLANES, 1), 0)
    conv = cbias_ref[...] + cw_ref[2:3, :] * z
    for shift in (1, 2):
        zs = pltpu.roll(z, shift, axis=0)
        head = jnp.where(top < shift, pltpu.roll(zh, shift, axis=0), zs[:SUBLANES])
        zs = jnp.concatenate([head, zs[SUBLANES:]], axis=0)
        conv = conv + cw_ref[2 - shift:3 - shift, :] * zs
    y_conv = cb_ref[...].astype(F32) * conv

    gate_a = jax.nn.sigmoid(ga_ref[...].astype(F32) + gbias_ref[0:1, :])
    gate_b = jax.nn.sigmoid(gb_ref[...].astype(F32) + gbias_ref[1:2, :])
    merged = gate_a * y_ref[...].astype(F32) + gate_b * y_conv

    r = jnp.dot(merged.astype(BF16), wout_ref[...], preferred_element_type=F32)
    h2 = h_ref[...] + _rms(r, gmix_ref[...])

    um = _rms(h2, gq_ref[...]).astype(BF16)
    qm = jnp.dot(um, wmq_ref[...], preferred_element_type=F32) * (MEM_HEAD_DIM ** -0.5)
    qm = qm.astype(BF16)
    nt = (((1,), (1,)), ((), ()))
    heads = []
    for hd in range(n_mem_heads):
        sl = slice(hd * MEM_HEAD_DIM, (hd + 1) * MEM_HEAD_DIM)
        s = lax.dot_general(qm[:, sl], km_ref[:, sl], nt, preferred_element_type=F32)
        p = jnp.exp(s - jnp.max(s, axis=-1, keepdims=True))
        l = jnp.sum(p, axis=-1, keepdims=True)
        o = jnp.dot(p.astype(BF16), vm_ref[:, sl], preferred_element_type=F32)
        heads.append((o / l).astype(BF16))
    om = jnp.concatenate(heads, axis=-1)
    mo = jnp.dot(om, wmo_ref[...], preferred_element_type=F32)
    o_ref[...] = h2 + _rms(mo, gmem_ref[...])


def _mix_out(h, y, p, kv, gate_bias, conv_w, conv_b, w_out, g_mix, g_q, w_mq, w_mo, g_mem,
             *, seq, n_mem, tm, first_col):
    n, d = h.shape
    mw = w_mq.shape[1]
    tiles_per_seq = seq // tm
    halo_blocks = tm // SUBLANES
    row = lambda c: pl.BlockSpec((tm, d), lambda i: (i, c))
    halo = lambda c: pl.BlockSpec(
        (SUBLANES, d), lambda i: (jnp.maximum(i * halo_blocks - 1, 0), c))
    const = lambda shape: pl.BlockSpec(shape, lambda i: (0, 0))
    body = functools.partial(_mix_out_kernel, tiles_per_seq=tiles_per_seq,
                             n_mem_heads=mw // MEM_HEAD_DIM)
    return pl.pallas_call(
        body,
        grid=(n // tm,),
        in_specs=[
            row(0), row(0),
            row(first_col), row(first_col + 1), row(first_col + 2),
            row(first_col + 3), row(first_col + 4),
            halo(first_col), halo(first_col + 2),
            const((2, d)), const((CONV_TAPS, d)), const((1, d)),
            const((d, d)), const((1, d)),
            const((1, d)), const((d, mw)),
            pl.BlockSpec((n_mem, mw), lambda i: (i // tiles_per_seq, 0)),
            pl.BlockSpec((n_mem, mw), lambda i: (i // tiles_per_seq, 1)),
            const((mw, d)), const((1, d)),
        ],
        out_specs=row(0),
        out_shape=jax.ShapeDtypeStruct((n, d), F32),
        compiler_params=_params("parallel"),
        name="mix_out",
    )(h, y, p, p, p, p, p, p, p, gate_bias, conv_w, conv_b, w_out, g_mix,
      g_q, w_mq, kv, kv, w_mo, g_mem)


def kernel(x, mem, ffn1_pre_g, ffn1_w13, ffn1_w2, ffn1_post_g, mix_pre_g, w_in, forget_bias,
           gate_bias, conv_w, conv_b, w_out, mix_post_g, mem_q_pre_g, mem_kv_g, w_mq, w_mkv,
           w_mo, mem_post_g, ffn2_pre_g, ffn2_w13, ffn2_w2, ffn2_post_g):
    batch, seq, d = x.shape
    n_mem = mem.shape[1]
    depth = ffn1_w13.shape[0]
    n_heads = forget_bias.shape[1]
    fox_w = n_heads * FOX_HEAD_DIM
    assert n_heads * AUG_PER_HEAD <= LANES and fox_w == d and conv_w.shape[2] == d

    row = lambda v: v.reshape(1, -1).astype(F32)
    h = x.reshape(batch * seq, d)
    mem2 = mem.reshape(batch * n_mem, d)
    for l in range(depth):
        h = _ffn(h, row(ffn1_pre_g[l]), ffn1_w13[l].astype(BF16), ffn1_w2[l].astype(BF16),
                 row(ffn1_post_g[l]), tm=512, tf=512)

        w = w_in[l]
        w_main = jnp.concatenate([w[:, :3 * fox_w], w[:, 3 * fox_w + n_heads:]], axis=1)
        w_f = jnp.pad(w[:, 3 * fox_w:3 * fox_w + n_heads], ((0, 0), (0, LANES - n_heads)))
        tn = 1024
        p, f = _norm_proj(h, row(mix_pre_g[l]), w_main.astype(BF16), tm=1024, tn=tn,
                          w_f=w_f.astype(BF16), n_scaled=fox_w // tn,
                          scale=FOX_HEAD_DIM ** -0.5, name="in_proj")
        fb = jnp.pad(forget_bias[l], (0, LANES - n_heads)).reshape(1, LANES)
        qa, ka = _forget_aug(f, fb, batch=batch, seq=seq, n_heads=n_heads)
        y = _fox_attention(p, qa, ka, batch=batch, seq=seq, n_heads=n_heads, tq=256)

        kv = _norm_proj(mem2, row(mem_kv_g[l]), w_mkv[l].astype(BF16), tm=1024,
                        tn=w_mkv.shape[2], name="mem_kv")
        h = _mix_out(h, y, p, kv, gate_bias[l], conv_w[l], row(conv_b[l]),
                     w_out[l].astype(BF16), row(mix_post_g[l]), row(mem_q_pre_g[l]),
                     w_mq[l].astype(BF16), w_mo[l].astype(BF16), row(mem_post_g[l]),
                     seq=seq, n_mem=n_mem, tm=256, first_col=3 * fox_w // d)

        h = _ffn(h, row(ffn2_pre_g[l]), ffn2_w13[l].astype(BF16), ffn2_w2[l].astype(BF16),
                 row(ffn2_post_g[l]), tm=512, tf=512)
    return h.reshape(batch, seq, d)
```

```python
import functools
import math

import jax
import jax.numpy as jnp
from jax import lax
from jax.experimental import pallas as pl
from jax.experimental.pallas import tpu as pltpu

EPS = 1e-6
FOX_HEAD_DIM = 128
MEM_HEAD_DIM = 128
CONV_TAPS = 3

LANES = 128
SUBLANES = 8
VMEM_LIMIT_BYTES = 56 * 1024 * 1024
AUG_PER_HEAD = 8
NEG_BIG = -1e30
LOG2E = math.log2(math.e)

BF16 = jnp.bfloat16
F32 = jnp.float32
NT_DIMS = (((1,), (1,)), ((), ()))


def _rms(x, g):
    ms = jnp.mean(x * x, axis=-1, keepdims=True)
    return x * lax.rsqrt(ms + EPS) * g


def _params(*sem):
    return pltpu.CompilerParams(dimension_semantics=sem, vmem_limit_bytes=VMEM_LIMIT_BYTES)


def _resident(shape):
    return pl.BlockSpec(shape, lambda *_: (0,) * len(shape), pipeline_mode=pl.Buffered(1))


def _ffn_kernel(*refs, has_xn, emit_next):
    refs = list(refs)
    x_ref = refs.pop(0)
    xin_ref = refs.pop(0) if has_xn else None
    gpre_ref = None if has_xn else refs.pop(0)
    w1_ref, w3_ref, w2_ref, gpost_ref = refs[:4]
    refs = refs[4:]
    gnext_ref = refs.pop(0) if emit_next else None
    o_ref = refs.pop(0)
    un_ref = refs.pop(0) if emit_next else None
    xn_ref = xin_ref if has_xn else refs.pop(0)
    j = pl.program_id(1)

    @pl.when(j == 0)
    def _():
        if not has_xn:
            xn_ref[...] = _rms(x_ref[...], gpre_ref[...]).astype(BF16)
        o_ref[...] = jnp.zeros_like(o_ref)

    xn = xn_ref[...]
    gate = jnp.dot(xn, w1_ref[...], preferred_element_type=F32)
    up = jnp.dot(xn, w3_ref[...], preferred_element_type=F32)
    act = (gate * jax.nn.sigmoid(gate) * up).astype(BF16)
    o_ref[...] += jnp.dot(act, w2_ref[...], preferred_element_type=F32)

    @pl.when(j == pl.num_programs(1) - 1)
    def _():
        h = x_ref[...] + 0.5 * _rms(o_ref[...], gpost_ref[...])
        o_ref[...] = h
        if emit_next:
            un_ref[...] = _rms(h, gnext_ref[...]).astype(BF16)


def _ffn(x, w13, w2, g_post, *, g_pre=None, xn=None, g_next=None, tm, tf):
    n, d = x.shape
    d_ff = w2.shape[0]
    nf = d_ff // tf
    has_xn, emit_next = xn is not None, g_next is not None
    rows = lambda dt: pl.BlockSpec((tm, d), lambda i, j: (i, 0))
    vec = pl.BlockSpec((1, d), lambda i, j: (0, 0))
    args, in_specs = [x], [rows(F32)]
    if has_xn:
        args.append(xn); in_specs.append(rows(BF16))
    else:
        args.append(g_pre); in_specs.append(vec)
    args += [w13, w13, w2, g_post]
    in_specs += [
        pl.BlockSpec((d, tf), lambda i, j: (0, j)),
        pl.BlockSpec((d, tf), lambda i, j: (0, j + nf)),
        pl.BlockSpec((tf, d), lambda i, j: (j, 0)),
        vec,
    ]
    out_specs, out_shape = rows(F32), jax.ShapeDtypeStruct((n, d), F32)
    if emit_next:
        args.append(g_next); in_specs.append(vec)
        out_specs = [out_specs, rows(BF16)]
        out_shape = [out_shape, jax.ShapeDtypeStruct((n, d), BF16)]
    return pl.pallas_call(
        functools.partial(_ffn_kernel, has_xn=has_xn, emit_next=emit_next),
        grid=(n // tm, nf),
        in_specs=in_specs,
        out_specs=out_specs,
        out_shape=out_shape,
        scratch_shapes=[] if has_xn else [pltpu.VMEM((tm, d), BF16)],
        compiler_params=_params("parallel", "arbitrary"),
        name="ffn",
    )(*args)


def _qkv_kernel(u_ref, w_ref, wf_ref, o_ref, f_ref, *, n_scaled, scale):
    j = pl.program_id(1)

    @pl.when(j == 0)
    def _():
        f_ref[...] = jnp.dot(u_ref[...], wf_ref[...], preferred_element_type=F32)

    acc = jnp.dot(u_ref[...], w_ref[...], preferred_element_type=F32)
    acc = acc * jnp.where(j < n_scaled, scale, 1.0).astype(F32)
    for hh in range(o_ref.shape[0]):
        o_ref[hh] = acc[:, hh * LANES:(hh + 1) * LANES].astype(BF16)


def _qkv_proj(u, w, w_f, *, batch, seq, tm, tn, q_scale):
    n, d = u.shape
    cols = w.shape[1]
    tiles_per_seq = seq // tm
    heads_per_blk = tn // FOX_HEAD_DIM
    body = functools.partial(_qkv_kernel, n_scaled=cols // 3 // tn, scale=q_scale)
    return pl.pallas_call(
        body,
        grid=(n // tm, cols // tn),
        in_specs=[
            pl.BlockSpec((tm, d), lambda i, j: (i, 0)),
            pl.BlockSpec((d, tn), lambda i, j: (0, j)),
            pl.BlockSpec((d, LANES), lambda i, j: (0, 0)),
        ],
        out_specs=[
            pl.BlockSpec((None, heads_per_blk, tm, FOX_HEAD_DIM),
                         lambda i, j: (i // tiles_per_seq, j, i % tiles_per_seq, 0)),
            pl.BlockSpec((tm, LANES), lambda i, j: (i, 0)),
        ],
        out_shape=[
            jax.ShapeDtypeStruct((batch, cols // FOX_HEAD_DIM, seq, FOX_HEAD_DIM), BF16),
            jax.ShapeDtypeStruct((n, LANES), F32),
        ],
        compiler_params=_params("parallel", "arbitrary"),
        name="qkv_proj",
    )(u, w, w_f)


def _conv_gate_kernel(u_ref, wcx_ref, wcb_ref, wcc_ref, wga_ref, wgb_ref, cw_ref, cbias_ref,
                      gbias_ref, yc_ref, ga_ref, halo_ref, *, tiles_per_seq):
    i, j = pl.program_id(0), pl.program_id(1)
    tm = u_ref.shape[0]

    @pl.when((i == 0) & (j == 0))
    def _():
        halo_ref[...] = jnp.zeros_like(halo_ref)

    u = u_ref[...]
    proj = lambda w_ref: jnp.dot(u, w_ref[...], preferred_element_type=F32)
    z = proj(wcc_ref) * proj(wcx_ref)
    zh = jnp.where(i % tiles_per_seq == 0, 0.0, halo_ref[j])
    halo_ref[j] = z[tm - SUBLANES:, :]

    w0, w1, w2 = cw_ref[0:1, :], cw_ref[1:2, :], cw_ref[2:3, :]
    conv = cbias_ref[...] + w2 * z + w1 * pltpu.roll(z, 1, axis=0) + w0 * pltpu.roll(z, 2, axis=0)
    top = lax.broadcasted_iota(jnp.int32, (SUBLANES, 1), 0)
    zt = z[:SUBLANES, :]
    conv_top = (cbias_ref[...] + w2 * zt
                + w1 * jnp.where(top < 1, pltpu.roll(zh, 1, axis=0), pltpu.roll(zt, 1, axis=0))
                + w0 * jnp.where(top < 2, pltpu.roll(zh, 2, axis=0), pltpu.roll(zt, 2, axis=0)))

    gated = jax.nn.sigmoid(proj(wgb_ref) + gbias_ref[1:2, :]) * proj(wcb_ref)
    yc_ref[...] = (gated * conv).astype(BF16)
    yc_ref[0:SUBLANES, :] = (gated[:SUBLANES, :] * conv_top).astype(BF16)
    ga_ref[...] = jax.nn.sigmoid(proj(wga_ref) + gbias_ref[0:1, :]).astype(BF16)


def _conv_gate(u, w, conv_w, conv_b, gate_bias, *, seq, tm, tn):
    n, d = u.shape
    width = w.shape[1] // 5
    nj = width // tn
    wspec = lambda g: pl.BlockSpec((d, tn), lambda i, j: (0, g * nj + j))
    cols = lambda r: pl.BlockSpec((r, tn), lambda i, j: (0, j))
    out = pl.BlockSpec((tm, tn), lambda i, j: (i, j))
    shape = jax.ShapeDtypeStruct((n, width), BF16)
    return pl.pallas_call(
        functools.partial(_conv_gate_kernel, tiles_per_seq=seq // tm),
        grid=(n // tm, nj),
        in_specs=[pl.BlockSpec((tm, d), lambda i, j: (i, 0)),
                  wspec(0), wspec(1), wspec(2), wspec(3), wspec(4),
                  cols(CONV_TAPS), cols(1), cols(2)],
        out_specs=[out, out],
        out_shape=[shape, shape],
        scratch_shapes=[pltpu.VMEM((nj, SUBLANES, tn), F32)],
        compiler_params=_params("arbitrary", "arbitrary"),
        name="conv_gate",
    )(u, w, w, w, w, w, conv_w, conv_b, gate_bias)


def _split3(x):
    hi = x.astype(BF16)
    r = x - hi.astype(F32)
    mid = r.astype(BF16)
    lo = (r - mid.astype(F32)).astype(BF16)
    return hi, mid, lo


def _forget_kernel(f_ref, b_ref, qa_ref, ka_ref, *, blk, n_heads):
    s = f_ref.shape[0]
    x = f_ref[...] + b_ref[...]
    logf = jnp.minimum(x, 0.0) - jnp.log1p(jnp.exp(-jnp.abs(x)))

    r = lax.broadcasted_iota(jnp.int32, (blk, blk), 0)
    c = lax.broadcasted_iota(jnp.int32, (blk, blk), 1)
    tri = (r >= c).astype(BF16)

    er = lax.broadcasted_iota(jnp.int32, (LANES, LANES), 0)
    ec = lax.broadcasted_iota(jnp.int32, (LANES, LANES), 1)
    lane = lax.broadcasted_iota(jnp.int32, (1, LANES), 1)
    slot = lane % AUG_PER_HEAD
    live = lane < n_heads * AUG_PER_HEAD

    def place(k):
        return ((ec == AUG_PER_HEAD * er + k) & (er < n_heads)).astype(BF16)

    q_ones = ((slot >= 3) & (slot < 6) & live).astype(F32)
    k_ones = ((slot < 3) & live).astype(F32)

    carry = jnp.zeros((1, LANES), F32)
    for b in range(s // blk):
        parts = _split3(logf[b * blk:(b + 1) * blk])
        local = sum(jnp.dot(tri, p, preferred_element_type=F32) for p in parts)
        cum = local + carry
        carry = cum[blk - 1:blk, :]
        c3 = _split3(cum * LOG2E)
        qa = sum(jnp.dot(c3[k], place(k), preferred_element_type=F32) for k in range(3))
        ka = sum(jnp.dot(c3[k], place(k + 3), preferred_element_type=F32) for k in range(3))
        qa_ref[b * blk:(b + 1) * blk, :] = (qa + q_ones).astype(BF16)
        ka_ref[b * blk:(b + 1) * blk, :] = (k_ones - ka).astype(BF16)


def _forget_aug(f, bias_row, *, batch, seq, n_heads):
    body = functools.partial(_forget_kernel, blk=256, n_heads=n_heads)
    spec = pl.BlockSpec((seq, LANES), lambda b: (b, 0))
    shape = jax.ShapeDtypeStruct((batch * seq, LANES), BF16)
    return pl.pallas_call(
        body,
        grid=(batch,),
        in_specs=[spec, pl.BlockSpec((1, LANES), lambda b: (0, 0))],
        out_specs=[spec, spec],
        out_shape=[shape, shape],
        compiler_params=_params("parallel"),
        name="forget_aug",
    )(f, bias_row)


def _attn_kernel(q_ref, k_ref, v_ref, qa_ref, ka_ref, o_ref, qc_ref, kc_ref, *, tq):
    s, dh = q_ref.shape
    h = pl.program_id(1)
    lane = lax.broadcasted_iota(jnp.int32, (1, LANES), 1)
    mine = (lane // AUG_PER_HEAD) == h
    qc_ref[:, :dh] = q_ref[...]
    qc_ref[:, dh:] = qa_ref[...]
    kc_ref[:, :dh] = k_ref[...]
    kc_ref[:, dh:] = jnp.where(mine, ka_ref[...], jnp.zeros_like(ka_ref[...]))

    row = lax.broadcasted_iota(jnp.int32, (tq, tq), 0)
    col = lax.broadcasted_iota(jnp.int32, (tq, tq), 1)
    causal = col <= row

    for i in range(s // tq):
        lo = i * tq
        qi = qc_ref[lo:lo + tq, :]
        sd = lax.dot_general(qi, kc_ref[lo:lo + tq, :], NT_DIMS, preferred_element_type=F32)
        sd = jnp.where(causal, sd, NEG_BIG)
        m = jnp.max(sd, axis=-1, keepdims=True)
        if i > 0:
            so = lax.dot_general(qi, kc_ref[0:lo, :], NT_DIMS, preferred_element_type=F32)
            m = jnp.maximum(m, jnp.max(so, axis=-1, keepdims=True))
        pd = jnp.exp2(sd - m)
        l = jnp.sum(pd, axis=-1, keepdims=True)
        acc = jnp.dot(pd.astype(BF16), v_ref[lo:lo + tq, :], preferred_element_type=F32)
        if i > 0:
            po = jnp.exp2(so - m)
            l = l + jnp.sum(po, axis=-1, keepdims=True)
            acc = acc + jnp.dot(po.astype(BF16), v_ref[0:lo, :], preferred_element_type=F32)
        o_ref[lo:lo + tq, :] = (acc / l).astype(o_ref.dtype)


def _fox_attention(qkv, qa, ka, *, n_heads, tq):
    batch, _, seq, dh = qkv.shape
    blk = lambda off: pl.BlockSpec((None, None, seq, dh), lambda b, h: (b, off + h, 0, 0))
    aug = pl.BlockSpec((seq, LANES), lambda b, h: (b, 0))
    return pl.pallas_call(
        functools.partial(_attn_kernel, tq=tq),
        grid=(batch, n_heads),
        in_specs=[blk(0), blk(n_heads), blk(2 * n_heads), aug, aug],
        out_specs=blk(0),
        out_shape=jax.ShapeDtypeStruct((batch, n_heads, seq, dh), BF16),
        scratch_shapes=[pltpu.VMEM((seq, dh + LANES), BF16),
                        pltpu.VMEM((seq, dh + LANES), BF16)],
        compiler_params=_params("parallel", "arbitrary"),
        name="fox_attn",
    )(qkv, qkv, qkv, qa, ka)


def _mix_out_kernel(h_ref, y_ref, ga_ref, yc_ref, wout_ref, gmix_ref, gq_ref, wmq_ref,
                    km_ref, vm_ref, wmo_ref, gmem_ref, gnext_ref, o_ref, un_ref):
    n_heads, _, dh = y_ref.shape
    parts = []
    for hd in range(n_heads):
        sl = slice(hd * dh, (hd + 1) * dh)
        part = ga_ref[:, sl].astype(F32) * y_ref[hd].astype(F32) + yc_ref[:, sl].astype(F32)
        parts.append(part.astype(BF16))
    merged = jnp.concatenate(parts, axis=-1)

    r = jnp.dot(merged, wout_ref[...], preferred_element_type=F32)
    h2 = h_ref[...] + _rms(r, gmix_ref[...])

    um = _rms(h2, gq_ref[...]).astype(BF16)
    qm = jnp.dot(um, wmq_ref[...], preferred_element_type=F32) * (MEM_HEAD_DIM ** -0.5 * LOG2E)
    qm = qm.astype(BF16)
    heads = []
    for hd in range(wmq_ref.shape[1] // MEM_HEAD_DIM):
        sl = slice(hd * MEM_HEAD_DIM, (hd + 1) * MEM_HEAD_DIM)
        s = lax.dot_general(qm[:, sl], km_ref[:, sl], NT_DIMS, preferred_element_type=F32)
        p = jnp.exp2(s - jnp.max(s, axis=-1, keepdims=True))
        l = jnp.sum(p, axis=-1, keepdims=True)
        o = jnp.dot(p.astype(BF16), vm_ref[:, sl], preferred_element_type=F32)
        heads.append((o / l).astype(BF16))
    om = jnp.concatenate(heads, axis=-1)
    mo = jnp.dot(om, wmo_ref[...], preferred_element_type=F32)
    h3 = h2 + _rms(mo, gmem_ref[...])
    o_ref[...] = h3
    un_ref[...] = _rms(h3, gnext_ref[...]).astype(BF16)


def _mix_out(h, y, ga, yc, kv, w_out, g_mix, g_q, w_mq, w_mo, g_mem, g_next, *, seq, tm):
    n, d = h.shape
    n_heads, dh = y.shape[1], y.shape[3]
    n_mem = kv.shape[0] // (n // seq)
    mw = w_mq.shape[1]
    tiles_per_seq = seq // tm
    rows = pl.BlockSpec((tm, d), lambda i: (i, 0))
    return pl.pallas_call(
        _mix_out_kernel,
        grid=(n // tm,),
        in_specs=[
            rows,
            pl.BlockSpec((None, n_heads, tm, dh),
                         lambda i: (i // tiles_per_seq, 0, i % tiles_per_seq, 0)),
            rows, rows,
            _resident((d, d)), _resident((1, d)),
            _resident((1, d)), _resident((d, mw)),
            pl.BlockSpec((n_mem, mw), lambda i: (i // tiles_per_seq, 0)),
            pl.BlockSpec((n_mem, mw), lambda i: (i // tiles_per_seq, 1)),
            _resident((mw, d)), _resident((1, d)), _resident((1, d)),
        ],
        out_specs=[rows, rows],
        out_shape=[jax.ShapeDtypeStruct((n, d), F32), jax.ShapeDtypeStruct((n, d), BF16)],
        compiler_params=_params("parallel"),
        name="mix_out",
    )(h, y, ga, yc, w_out, g_mix, g_q, w_mq, kv, kv, w_mo, g_mem, g_next)


def _mem_kv_kernel(x_ref, g_ref, w_ref, o_ref):
    xn = _rms(x_ref[...], g_ref[...]).astype(BF16)
    o_ref[...] = jnp.dot(xn, w_ref[...], preferred_element_type=F32).astype(BF16)


def _mem_kv(mem, g, w, *, tm):
    n, d = mem.shape
    cols = w.shape[1]
    return pl.pallas_call(
        _mem_kv_kernel,
        grid=(n // tm,),
        in_specs=[pl.BlockSpec((tm, d), lambda i: (i, 0)), _resident((1, d)),
                  _resident((d, cols))],
        out_specs=pl.BlockSpec((tm, cols), lambda i: (i, 0)),
        out_shape=jax.ShapeDtypeStruct((n, cols), BF16),
        compiler_params=_params("parallel"),
        name="mem_kv",
    )(mem, g, w)


def kernel(x, mem, ffn1_pre_g, ffn1_w13, ffn1_w2, ffn1_post_g, mix_pre_g, w_in, forget_bias,
           gate_bias, conv_w, conv_b, w_out, mix_post_g, mem_q_pre_g, mem_kv_g, w_mq, w_mkv,
           w_mo, mem_post_g, ffn2_pre_g, ffn2_w13, ffn2_w2, ffn2_post_g):
    batch, seq, d = x.shape
    n_mem = mem.shape[1]
    depth = ffn1_w13.shape[0]
    n_heads = forget_bias.shape[1]
    fox_w = n_heads * FOX_HEAD_DIM
    assert n_heads * AUG_PER_HEAD <= LANES and fox_w == d and conv_w.shape[2] == d

    row = lambda v: v.reshape(1, -1).astype(F32)
    h = x.reshape(batch * seq, d)
    mem2 = mem.reshape(batch * n_mem, d)
    for l in range(depth):
        h, u = _ffn(h, ffn1_w13[l].astype(BF16), ffn1_w2[l].astype(BF16), row(ffn1_post_g[l]),
                    g_pre=row(ffn1_pre_g[l]), g_next=row(mix_pre_g[l]), tm=512, tf=512)

        w = w_in[l].astype(BF16)
        w_f = jnp.pad(w[:, 3 * fox_w:3 * fox_w + n_heads], ((0, 0), (0, LANES - n_heads)))
        qkv, f = _qkv_proj(u, w[:, :3 * fox_w], w_f, batch=batch, seq=seq, tm=1024, tn=1024,
                           q_scale=FOX_HEAD_DIM ** -0.5 * LOG2E)
        yc, ga = _conv_gate(u, w[:, 3 * fox_w + n_heads:], conv_w[l], row(conv_b[l]),
                            gate_bias[l], seq=seq, tm=1024, tn=256)
        fb = jnp.pad(forget_bias[l], (0, LANES - n_heads)).reshape(1, LANES)
        qa, ka = _forget_aug(f, fb, batch=batch, seq=seq, n_heads=n_heads)
        y = _fox_attention(qkv, qa, ka, n_heads=n_heads, tq=256)

        kv = _mem_kv(mem2, row(mem_kv_g[l]), w_mkv[l].astype(BF16), tm=1024)
        h, u = _mix_out(h, y, ga, yc, kv, w_out[l].astype(BF16), row(mix_post_g[l]),
                        row(mem_q_pre_g[l]), w_mq[l].astype(BF16), w_mo[l].astype(BF16),
                        row(mem_post_g[l]), row(ffn2_pre_g[l]), seq=seq, tm=512)

        h = _ffn(h, ffn2_w13[l].astype(BF16), ffn2_w2[l].astype(BF16), row(ffn2_post_g[l]),
                 xn=u, tm=512, tf=512)
    return h.reshape(batch, seq, d)
```

```python
import functools
import math

import jax
import jax.numpy as jnp
from jax import lax
from jax.experimental import pallas as pl
from jax.experimental.pallas import tpu as pltpu

EPS = 1e-6
FOX_HEAD_DIM = 128
MEM_HEAD_DIM = 128
CONV_TAPS = 3

LANES = 128
SUBLANES = 8
VMEM_LIMIT_BYTES = 56 * 1024 * 1024
AUG_PER_HEAD = 8
NEG_BIG = -1e30
LOG2E = math.log2(math.e)

BF16 = jnp.bfloat16
F32 = jnp.float32
NT_DIMS = (((1,), (1,)), ((), ()))


def _rms(x, g):
    ms = jnp.mean(x * x, axis=-1, keepdims=True)
    return x * lax.rsqrt(ms + EPS) * g


def _params(*sem):
    return pltpu.CompilerParams(dimension_semantics=sem, vmem_limit_bytes=VMEM_LIMIT_BYTES)


def _resident(shape):
    return pl.BlockSpec(shape, lambda *_: (0,) * len(shape), pipeline_mode=pl.Buffered(1))


def _ffn_kernel(*refs, has_xn, emit_next):
    refs = list(refs)
    x_ref = refs.pop(0)
    xin_ref = refs.pop(0) if has_xn else None
    gpre_ref = None if has_xn else refs.pop(0)
    w1_ref, w3_ref, w2_ref, gpost_ref = refs[:4]
    refs = refs[4:]
    gnext_ref = refs.pop(0) if emit_next else None
    o_ref = refs.pop(0)
    un_ref = refs.pop(0) if emit_next else None
    xn_ref = xin_ref if has_xn else refs.pop(0)
    j = pl.program_id(1)

    @pl.when(j == 0)
    def _():
        if not has_xn:
            xn_ref[...] = _rms(x_ref[...], gpre_ref[...]).astype(BF16)
        o_ref[...] = jnp.zeros_like(o_ref)

    xn = xn_ref[...]
    gate = jnp.dot(xn, w1_ref[...], preferred_element_type=F32)
    up = jnp.dot(xn, w3_ref[...], preferred_element_type=F32)
    act = (gate * jax.nn.sigmoid(gate) * up).astype(BF16)
    o_ref[...] += jnp.dot(act, w2_ref[...], preferred_element_type=F32)

    @pl.when(j == pl.num_programs(1) - 1)
    def _():
        h = x_ref[...] + _rms(o_ref[...], gpost_ref[...])
        o_ref[...] = h
        if emit_next:
            un_ref[...] = _rms(h, gnext_ref[...]).astype(BF16)


def _ffn(x, w13, w2, g_post, *, g_pre=None, xn=None, g_next=None, tm, tf):
    n, d = x.shape
    d_ff = w2.shape[0]
    nf = d_ff // tf
    has_xn, emit_next = xn is not None, g_next is not None
    rows = lambda dt: pl.BlockSpec((tm, d), lambda i, j: (i, 0))
    vec = pl.BlockSpec((1, d), lambda i, j: (0, 0))
    args, in_specs = [x], [rows(F32)]
    if has_xn:
        args.append(xn); in_specs.append(rows(BF16))
    else:
        args.append(g_pre); in_specs.append(vec)
    args += [w13, w13, w2, g_post]
    in_specs += [
        pl.BlockSpec((d, tf), lambda i, j: (0, j)),
        pl.BlockSpec((d, tf), lambda i, j: (0, j + nf)),
        pl.BlockSpec((tf, d), lambda i, j: (j, 0)),
        vec,
    ]
    out_specs, out_shape = rows(F32), jax.ShapeDtypeStruct((n, d), F32)
    if emit_next:
        args.append(g_next); in_specs.append(vec)
        out_specs = [out_specs, rows(BF16)]
        out_shape = [out_shape, jax.ShapeDtypeStruct((n, d), BF16)]
    return pl.pallas_call(
        functools.partial(_ffn_kernel, has_xn=has_xn, emit_next=emit_next),
        grid=(n // tm, nf),
        in_specs=in_specs,
        out_specs=out_specs,
        out_shape=out_shape,
        scratch_shapes=[] if has_xn else [pltpu.VMEM((tm, d), BF16)],
        compiler_params=_params("parallel", "arbitrary"),
        name="ffn",
    )(*args)


def _qkv_kernel(u_ref, w_ref, wf_ref, o_ref, f_ref, *, n_scaled, scale):
    j = pl.program_id(1)

    @pl.when(j == 0)
    def _():
        f_ref[...] = jnp.dot(u_ref[...], wf_ref[...], preferred_element_type=F32)

    acc = jnp.dot(u_ref[...], w_ref[...], preferred_element_type=F32)
    acc = acc * jnp.where(j < n_scaled, scale, 1.0).astype(F32)
    for hh in range(o_ref.shape[0]):
        o_ref[hh] = acc[:, hh * LANES:(hh + 1) * LANES].astype(BF16)


def _qkv_proj(u, w, w_f, *, batch, seq, tm, tn, q_scale):
    n, d = u.shape
    cols = w.shape[1]
    tiles_per_seq = seq // tm
    heads_per_blk = tn // FOX_HEAD_DIM
    body = functools.partial(_qkv_kernel, n_scaled=cols // 3 // tn, scale=q_scale)
    return pl.pallas_call(
        body,
        grid=(n // tm, cols // tn),
        in_specs=[
            pl.BlockSpec((tm, d), lambda i, j: (i, 0)),
            pl.BlockSpec((d, tn), lambda i, j: (0, j)),
            pl.BlockSpec((d, LANES), lambda i, j: (0, 0)),
        ],
        out_specs=[
            pl.BlockSpec((None, heads_per_blk, tm, FOX_HEAD_DIM),
                         lambda i, j: (i // tiles_per_seq, j, i % tiles_per_seq, 0)),
            pl.BlockSpec((tm, LANES), lambda i, j: (i, 0)),
        ],
        out_shape=[
            jax.ShapeDtypeStruct((batch, cols // FOX_HEAD_DIM, seq, FOX_HEAD_DIM), BF16),
            jax.ShapeDtypeStruct((n, LANES), F32),
        ],
        compiler_params=_params("parallel", "arbitrary"),
        name="qkv_proj",
    )(u, w, w_f)


def _conv_gate_kernel(u_ref, wcx_ref, wcb_ref, wcc_ref, wga_ref, wgb_ref, cw_ref, cbias_ref,
                      gbias_ref, yc_ref, ga_ref, halo_ref, *, tiles_per_seq):
    i, j = pl.program_id(0), pl.program_id(1)
    tm = u_ref.shape[0]

    @pl.when((i == 0) & (j == 0))
    def _():
        halo_ref[...] = jnp.zeros_like(halo_ref)

    u = u_ref[...]
    proj = lambda w_ref: jnp.dot(u, w_ref[...], preferred_element_type=F32)
    z = proj(wcc_ref) * proj(wcx_ref)
    zh = jnp.where(i % tiles_per_seq == 0, 0.0, halo_ref[j])
    halo_ref[j] = z[tm - SUBLANES:, :]

    w0, w1, w2 = cw_ref[0:1, :], cw_ref[1:2, :], cw_ref[2:3, :]
    conv = cbias_ref[...] + w2 * z + w1 * pltpu.roll(z, 1, axis=0) + w0 * pltpu.roll(z, 2, axis=0)
    top = lax.broadcasted_iota(jnp.int32, (SUBLANES, 1), 0)
    zt = z[:SUBLANES, :]
    conv_top = (cbias_ref[...] + w2 * zt
                + w1 * jnp.where(top < 1, pltpu.roll(zh, 1, axis=0), pltpu.roll(zt, 1, axis=0))
                + w0 * jnp.where(top < 2, pltpu.roll(zh, 2, axis=0), pltpu.roll(zt, 2, axis=0)))

    gated = jax.nn.sigmoid(proj(wgb_ref) + gbias_ref[1:2, :]) * proj(wcb_ref)
    yc_ref[...] = (gated * conv).astype(BF16)
    yc_ref[0:SUBLANES, :] = (gated[:SUBLANES, :] * conv_top).astype(BF16)
    ga_ref[...] = jax.nn.sigmoid(proj(wga_ref) + gbias_ref[0:1, :]).astype(BF16)


def _conv_gate(u, w, conv_w, conv_b, gate_bias, *, seq, tm, tn):
    n, d = u.shape
    width = w.shape[1] // 5
    nj = width // tn
    wspec = lambda g: pl.BlockSpec((d, tn), lambda i, j: (0, g * nj + j))
    cols = lambda r: pl.BlockSpec((r, tn), lambda i, j: (0, j))
    out = pl.BlockSpec((tm, tn), lambda i, j: (i, j))
    shape = jax.ShapeDtypeStruct((n, width), BF16)
    return pl.pallas_call(
        functools.partial(_conv_gate_kernel, tiles_per_seq=seq // tm),
        grid=(n // tm, nj),
        in_specs=[pl.BlockSpec((tm, d), lambda i, j: (i, 0)),
                  wspec(0), wspec(1), wspec(2), wspec(3), wspec(4),
                  cols(CONV_TAPS), cols(1), cols(2)],
        out_specs=[out, out],
        out_shape=[shape, shape],
        scratch_shapes=[pltpu.VMEM((nj, SUBLANES, tn), F32)],
        compiler_params=_params("arbitrary", "arbitrary"),
        name="conv_gate",
    )(u, w, w, w, w, w, conv_w, conv_b, gate_bias)


def _split3(x):
    hi = x.astype(BF16)
    r = x - hi.astype(F32)
    mid = r.astype(BF16)
    lo = (r - mid.astype(F32)).astype(BF16)
    return hi, mid, lo


def _forget_kernel(f_ref, b_ref, qa_ref, ka_ref, *, blk, n_heads):
    s = f_ref.shape[0]
    x = f_ref[...] + b_ref[...]
    logf = jnp.minimum(x, 0.0) - jnp.log1p(jnp.exp(-jnp.abs(x)))

    r = lax.broadcasted_iota(jnp.int32, (blk, blk), 0)
    c = lax.broadcasted_iota(jnp.int32, (blk, blk), 1)
    tri = (r >= c).astype(BF16)

    er = lax.broadcasted_iota(jnp.int32, (LANES, LANES), 0)
    ec = lax.broadcasted_iota(jnp.int32, (LANES, LANES), 1)
    lane = lax.broadcasted_iota(jnp.int32, (1, LANES), 1)
    slot = lane % AUG_PER_HEAD
    live = lane < n_heads * AUG_PER_HEAD

    def place(k):
        return ((ec == AUG_PER_HEAD * er + k) & (er < n_heads)).astype(BF16)

    q_ones = ((slot >= 3) & (slot < 6) & live).astype(F32)
    k_ones = ((slot < 3) & live).astype(F32)

    carry = jnp.zeros((1, LANES), F32)
    for b in range(s // blk):
        parts = _split3(logf[b * blk:(b + 1) * blk])
        local = sum(jnp.dot(tri, p, preferred_element_type=F32) for p in parts)
        cum = local + carry
        carry = cum[blk - 1:blk, :]
        c3 = _split3(cum * LOG2E)
        qa = sum(jnp.dot(c3[k], place(k), preferred_element_type=F32) for k in range(3))
        ka = sum(jnp.dot(c3[k], place(k + 3), preferred_element_type=F32) for k in range(3))
        qa_ref[b * blk:(b + 1) * blk, :] = (qa + q_ones).astype(BF16)
        ka_ref[b * blk:(b + 1) * blk, :] = (k_ones - ka).astype(BF16)


def _forget_aug(f, bias_row, *, batch, seq, n_heads):
    body = functools.partial(_forget_kernel, blk=256, n_heads=n_heads)
    spec = pl.BlockSpec((seq, LANES), lambda b: (b, 0))
    shape = jax.ShapeDtypeStruct((batch * seq, LANES), BF16)
    return pl.pallas_call(
        body,
        grid=(batch,),
        in_specs=[spec, pl.BlockSpec((1, LANES), lambda b: (0, 0))],
        out_specs=[spec, spec],
        out_shape=[shape, shape],
        compiler_params=_params("parallel"),
        name="forget_aug",
    )(f, bias_row)


def _attn_kernel(q_ref, k_ref, v_ref, qa_ref, ka_ref, o_ref, qc_ref, kc_ref, vc_ref, *, tq):
    hp, s, dh = q_ref.shape
    heads = range(hp)
    lane = lax.broadcasted_iota(jnp.int32, (1, LANES), 1)
    ones_col = jnp.broadcast_to((lane == 0).astype(BF16), (s, LANES))
    for a in heads:
        mine = (lane // AUG_PER_HEAD) == pl.program_id(1) * hp + a
        qc_ref[a, :, :dh] = q_ref[a]
        qc_ref[a, :, dh:] = qa_ref[...]
        kc_ref[a, :, :dh] = k_ref[a]
        kc_ref[a, :, dh:] = jnp.where(mine, ka_ref[...], jnp.zeros_like(ka_ref[...]))
        vc_ref[a, :, :dh] = v_ref[a]
        vc_ref[a, :, dh:] = ones_col

    row = lax.broadcasted_iota(jnp.int32, (tq, tq), 0)
    col = lax.broadcasted_iota(jnp.int32, (tq, tq), 1)
    causal = col <= row
    qk = lambda x, y: lax.dot_general(x, y, NT_DIMS, preferred_element_type=F32)
    pv = lambda x, y: jnp.dot(x.astype(BF16), y, preferred_element_type=F32)
    rowmax = lambda x: jnp.max(x, axis=-1, keepdims=True)

    for i in reversed(range(s // tq)):
        lo = i * tq
        qi = [qc_ref[a, lo:lo + tq, :] for a in heads]
        sd = [jnp.where(causal, qk(qi[a], kc_ref[a, lo:lo + tq, :]), NEG_BIG) for a in heads]
        m = [rowmax(x) for x in sd]
        if i > 0:
            so = [qk(qi[a], kc_ref[a, 0:lo, :]) for a in heads]
            m = [jnp.maximum(m[a], rowmax(so[a])) for a in heads]
        acc = [pv(jnp.exp2(sd[a] - m[a]), vc_ref[a, lo:lo + tq, :]) for a in heads]
        if i > 0:
            acc = [acc[a] + pv(jnp.exp2(so[a] - m[a]), vc_ref[a, 0:lo, :]) for a in heads]
        for a in heads:
            o_ref[a, lo:lo + tq, :] = (acc[a][:, :dh] / acc[a][:, dh:dh + 1]).astype(o_ref.dtype)


def _fox_attention(qkv, qa, ka, *, n_heads, tq, hp):
    batch, _, seq, dh = qkv.shape
    groups = n_heads // hp
    blk = lambda off: pl.BlockSpec((None, hp, seq, dh), lambda b, g: (b, off + g, 0, 0))
    aug = pl.BlockSpec((seq, LANES), lambda b, g: (b, 0))
    return pl.pallas_call(
        functools.partial(_attn_kernel, tq=tq),
        grid=(batch, groups),
        in_specs=[blk(0), blk(groups), blk(2 * groups), aug, aug],
        out_specs=blk(0),
        out_shape=jax.ShapeDtypeStruct((batch, n_heads, seq, dh), BF16),
        scratch_shapes=[pltpu.VMEM((hp, seq, dh + LANES), BF16)] * 3,
        compiler_params=_params("parallel", "arbitrary"),
        name="fox_attn",
    )(qkv, qkv, qkv, qa, ka)


def _mix_out_kernel(h_ref, y_ref, ga_ref, yc_ref, wout_ref, gmix_ref, gq_ref, wmq_ref,
                    km_ref, vm_ref, wmo_ref, gmem_ref, gnext_ref, o_ref, un_ref):
    n_heads, _, dh = y_ref.shape
    parts = []
    for hd in range(n_heads):
        sl = slice(hd * dh, (hd + 1) * dh)
        part = ga_ref[:, sl].astype(F32) * y_ref[hd].astype(F32) + yc_ref[:, sl].astype(F32)
        parts.append(part.astype(BF16))
    merged = jnp.concatenate(parts, axis=-1)

    r = jnp.dot(merged, wout_ref[...], preferred_element_type=F32)
    h2 = h_ref[...] + _rms(r, gmix_ref[...])

    um = _rms(h2, gq_ref[...]).astype(BF16)
    qm = jnp.dot(um, wmq_ref[...], preferred_element_type=F32) * (MEM_HEAD_DIM ** -0.5 * LOG2E)
    qm = qm.astype(BF16)
    heads = []
    for hd in range(wmq_ref.shape[1] // MEM_HEAD_DIM):
        sl = slice(hd * MEM_HEAD_DIM, (hd + 1) * MEM_HEAD_DIM)
        s = lax.dot_general(qm[:, sl], km_ref[:, sl], NT_DIMS, preferred_element_type=F32)
        p = jnp.exp2(s - jnp.max(s, axis=-1, keepdims=True))
        l = jnp.sum(p, axis=-1, keepdims=True)
        o = jnp.dot(p.astype(BF16), vm_ref[:, sl], preferred_element_type=F32)
        heads.append((o / l).astype(BF16))
    om = jnp.concatenate(heads, axis=-1)
    mo = jnp.dot(om, wmo_ref[...], preferred_element_type=F32)
    h3 = h2 + _rms(mo, gmem_ref[...])
    o_ref[...] = h3
    un_ref[...] = _rms(h3, gnext_ref[...]).astype(BF16)


def _mix_out(h, y, ga, yc, kv, w_out, g_mix, g_q, w_mq, w_mo, g_mem, g_next, *, seq, tm):
    n, d = h.shape
    n_heads, dh = y.shape[1], y.shape[3]
    n_mem = kv.shape[0] // (n // seq)
    mw = w_mq.shape[1]
    tiles_per_seq = seq // tm
    rows = pl.BlockSpec((tm, d), lambda i: (i, 0))
    return pl.pallas_call(
        _mix_out_kernel,
        grid=(n // tm,),
        in_specs=[
            rows,
            pl.BlockSpec((None, n_heads, tm, dh),
                         lambda i: (i // tiles_per_seq, 0, i % tiles_per_seq, 0)),
            rows, rows,
            _resident((d, d)), _resident((1, d)),
            _resident((1, d)), _resident((d, mw)),
            pl.BlockSpec((n_mem, mw), lambda i: (i // tiles_per_seq, 0)),
            pl.BlockSpec((n_mem, mw), lambda i: (i // tiles_per_seq, 1)),
            _resident((mw, d)), _resident((1, d)), _resident((1, d)),
        ],
        out_specs=[rows, rows],
        out_shape=[jax.ShapeDtypeStruct((n, d), F32), jax.ShapeDtypeStruct((n, d), BF16)],
        compiler_params=_params("parallel"),
        name="mix_out",
    )(h, y, ga, yc, w_out, g_mix, g_q, w_mq, kv, kv, w_mo, g_mem, g_next)


def _mem_kv_kernel(x_ref, g_ref, w_ref, o_ref):
    xn = _rms(x_ref[...], g_ref[...]).astype(BF16)
    o_ref[...] = jnp.dot(xn, w_ref[...], preferred_element_type=F32).astype(BF16)


def _mem_kv(mem, g, w, *, tm):
    n, d = mem.shape
    cols = w.shape[1]
    return pl.pallas_call(
        _mem_kv_kernel,
        grid=(n // tm,),
        in_specs=[pl.BlockSpec((tm, d), lambda i: (i, 0)), _resident((1, d)),
                  _resident((d, cols))],
        out_specs=pl.BlockSpec((tm, cols), lambda i: (i, 0)),
        out_shape=jax.ShapeDtypeStruct((n, cols), BF16),
        compiler_params=_params("parallel"),
        name="mem_kv",
    )(mem, g, w)


def kernel(x, mem, ffn1_pre_g, ffn1_w13, ffn1_w2, ffn1_post_g, mix_pre_g, w_in, forget_bias,
           gate_bias, conv_w, conv_b, w_out, mix_post_g, mem_q_pre_g, mem_kv_g, w_mq, w_mkv,
           w_mo, mem_post_g, ffn2_pre_g, ffn2_w13, ffn2_w2, ffn2_post_g):
    batch, seq, d = x.shape
    n_mem = mem.shape[1]
    depth = ffn1_w13.shape[0]
    n_heads = forget_bias.shape[1]
    fox_w = n_heads * FOX_HEAD_DIM
    assert n_heads * AUG_PER_HEAD <= LANES and fox_w == d and conv_w.shape[2] == d

    row = lambda v: v.reshape(1, -1).astype(F32)
    h = x.reshape(batch * seq, d)
    mem2 = mem.reshape(batch * n_mem, d)
    for l in range(depth):
        h, u = _ffn(h, ffn1_w13[l].astype(BF16), ffn1_w2[l].astype(BF16),
                    0.5 * row(ffn1_post_g[l]), g_pre=row(ffn1_pre_g[l]),
                    g_next=row(mix_pre_g[l]), tm=512, tf=512)

        w = w_in[l]
        w_qkv = w[:, :3 * fox_w].astype(BF16)
        w_f = jnp.pad(w[:, 3 * fox_w:3 * fox_w + n_heads].astype(BF16),
                      ((0, 0), (0, LANES - n_heads)))
        w_cg = w[:, 3 * fox_w + n_heads:].astype(BF16)
        qkv, f = _qkv_proj(u, w_qkv, w_f, batch=batch, seq=seq, tm=1024, tn=1024,
                           q_scale=FOX_HEAD_DIM ** -0.5 * LOG2E)
        yc, ga = _conv_gate(u, w_cg, conv_w[l], row(conv_b[l]),
                            gate_bias[l], seq=seq, tm=1024, tn=256)
        fb = jnp.pad(forget_bias[l], (0, LANES - n_heads)).reshape(1, LANES)
        qa, ka = _forget_aug(f, fb, batch=batch, seq=seq, n_heads=n_heads)
        y = _fox_attention(qkv, qa, ka, n_heads=n_heads, tq=256, hp=4)

        kv = _mem_kv(mem2, row(mem_kv_g[l]), w_mkv[l].astype(BF16), tm=1024)
        h, u = _mix_out(h, y, ga, yc, kv, w_out[l].astype(BF16), row(mix_post_g[l]),
                        row(mem_q_pre_g[l]), w_mq[l].astype(BF16), w_mo[l].astype(BF16),
                        row(mem_post_g[l]), row(ffn2_pre_g[l]), seq=seq, tm=512)

        h = _ffn(h, ffn2_w13[l].astype(BF16), ffn2_w2[l].astype(BF16),
                 0.5 * row(ffn2_post_g[l]), xn=u, tm=512, tf=512)
    return h.reshape(batch, seq, d)
```

```python
import functools
import math

import jax
import jax.numpy as jnp
from jax import lax
from jax.experimental import pallas as pl
from jax.experimental.pallas import tpu as pltpu

EPS = 1e-6
FOX_HEAD_DIM = 128
MEM_HEAD_DIM = 128
CONV_TAPS = 3

LANES = 128
SUBLANES = 8
VMEM_LIMIT_BYTES = 56 * 1024 * 1024
AUG_PER_HEAD = 8
NEG_BIG = -1e30
LOG2E = math.log2(math.e)

BF16 = jnp.bfloat16
F32 = jnp.float32
NT_DIMS = (((1,), (1,)), ((), ()))


def _rms(x, g):
    ms = jnp.mean(x * x, axis=-1, keepdims=True)
    return x * lax.rsqrt(ms + EPS) * g


def _params(*sem):
    return pltpu.CompilerParams(dimension_semantics=sem, vmem_limit_bytes=VMEM_LIMIT_BYTES)


def _resident(shape):
    return pl.BlockSpec(shape, lambda *_: (0,) * len(shape), pipeline_mode=pl.Buffered(1))


def _ffn_kernel(*refs, has_xn, emit_next, n_pairs):
    refs = list(refs)
    x_ref = refs.pop(0)
    xin_ref = refs.pop(0) if has_xn else None
    gpre_ref = None if has_xn else refs.pop(0)
    w1a_ref, w3a_ref, w2a_ref, w1b_ref, w3b_ref, w2b_ref, gpost_ref = refs[:7]
    refs = refs[7:]
    gnext_ref = refs.pop(0) if emit_next else None
    o_ref = refs.pop(0)
    un_ref = refs.pop(0) if emit_next else None
    xn_ref = xin_ref if has_xn else refs.pop(0)
    j = pl.program_id(1)

    def chunk(xn, w1_ref, w3_ref, w2_ref):
        gate = jnp.dot(xn, w1_ref[...], preferred_element_type=F32)
        up = jnp.dot(xn, w3_ref[...], preferred_element_type=F32)
        act = (gate * jax.nn.sigmoid(gate) * up).astype(BF16)
        return jnp.dot(act, w2_ref[...], preferred_element_type=F32)

    def pair(xn):
        return chunk(xn, w1a_ref, w3a_ref, w2a_ref) + chunk(xn, w1b_ref, w3b_ref, w2b_ref)

    @pl.when(j == 0)
    def _():
        if has_xn:
            xn = xn_ref[...]
        else:
            xn = _rms(x_ref[...], gpre_ref[...]).astype(BF16)
            xn_ref[...] = xn
        o_ref[...] = pair(xn)

    @pl.when((j > 0) & (j < n_pairs))
    def _():
        o_ref[...] += pair(xn_ref[...])

    @pl.when(j == n_pairs)
    def _():
        f = o_ref[...] + chunk(xn_ref[...], w1a_ref, w3a_ref, w2a_ref)
        h = x_ref[...] + _rms(f, gpost_ref[...])
        o_ref[...] = h
        if emit_next:
            un_ref[...] = _rms(h, gnext_ref[...]).astype(BF16)


def _ffn(x, w13, w2, g_post, *, g_pre=None, xn=None, g_next=None, tm, tf):
    n, d = x.shape
    d_ff = w2.shape[0]
    nf = d_ff // tf
    assert nf % 2 == 1 and nf >= 3, "d_ff chunks are walked as pairs plus one"
    n_pairs = nf // 2
    chunk_a = lambda j: jnp.minimum(2 * j, nf - 1)
    chunk_b = lambda j: jnp.minimum(2 * j + 1, nf - 1)
    has_xn, emit_next = xn is not None, g_next is not None
    rows = pl.BlockSpec((tm, d), lambda i, j: (i, 0))
    vec = pl.BlockSpec((1, d), lambda i, j: (0, 0))
    args, in_specs = [x], [rows]
    if has_xn:
        args.append(xn); in_specs.append(rows)
    else:
        args.append(g_pre); in_specs.append(vec)
    for chunk in (chunk_a, chunk_b):
        args += [w13, w13, w2]
        in_specs += [
            pl.BlockSpec((d, tf), lambda i, j, c=chunk: (0, c(j))),
            pl.BlockSpec((d, tf), lambda i, j, c=chunk: (0, c(j) + nf)),
            pl.BlockSpec((tf, d), lambda i, j, c=chunk: (c(j), 0)),
        ]
    args.append(g_post); in_specs.append(vec)
    out_specs, out_shape = rows, jax.ShapeDtypeStruct((n, d), F32)
    if emit_next:
        args.append(g_next); in_specs.append(vec)
        out_specs = [out_specs, rows]
        out_shape = [out_shape, jax.ShapeDtypeStruct((n, d), BF16)]
    body = functools.partial(_ffn_kernel, has_xn=has_xn, emit_next=emit_next, n_pairs=n_pairs)
    return pl.pallas_call(
        body,
        grid=(n // tm, n_pairs + 1),
        in_specs=in_specs,
        out_specs=out_specs,
        out_shape=out_shape,
        scratch_shapes=[] if has_xn else [pltpu.VMEM((tm, d), BF16)],
        compiler_params=_params("parallel", "arbitrary"),
        name="ffn",
    )(*args)


def _qkv_kernel(u_ref, w_ref, wf_ref, o_ref, f_ref, *, n_scaled, scale):
    j = pl.program_id(1)

    @pl.when(j == 0)
    def _():
        f_ref[...] = jnp.dot(u_ref[...], wf_ref[...], preferred_element_type=F32)

    acc = jnp.dot(u_ref[...], w_ref[...], preferred_element_type=F32)
    acc = acc * jnp.where(j < n_scaled, scale, 1.0).astype(F32)
    for hh in range(o_ref.shape[0]):
        o_ref[hh] = acc[:, hh * LANES:(hh + 1) * LANES].astype(BF16)


def _qkv_proj(u, w, w_f, *, batch, seq, tm, tn, q_scale):
    n, d = u.shape
    cols = w.shape[1]
    tiles_per_seq = seq // tm
    heads_per_blk = tn // FOX_HEAD_DIM
    body = functools.partial(_qkv_kernel, n_scaled=cols // 3 // tn, scale=q_scale)
    return pl.pallas_call(
        body,
        grid=(n // tm, cols // tn),
        in_specs=[
            pl.BlockSpec((tm, d), lambda i, j: (i, 0)),
            pl.BlockSpec((d, tn), lambda i, j: (0, j)),
            pl.BlockSpec((d, LANES), lambda i, j: (0, 0)),
        ],
        out_specs=[
            pl.BlockSpec((None, heads_per_blk, tm, FOX_HEAD_DIM),
                         lambda i, j: (i // tiles_per_seq, j, i % tiles_per_seq, 0)),
            pl.BlockSpec((tm, LANES), lambda i, j: (i, 0)),
        ],
        out_shape=[
            jax.ShapeDtypeStruct((batch, cols // FOX_HEAD_DIM, seq, FOX_HEAD_DIM), BF16),
            jax.ShapeDtypeStruct((n, LANES), F32),
        ],
        compiler_params=_params("parallel", "arbitrary"),
        name="qkv_proj",
    )(u, w, w_f)


def _conv_gate_kernel(u_ref, wcx_ref, wcb_ref, wcc_ref, wga_ref, wgb_ref, cw_ref, cbias_ref,
                      gbias_ref, yc_ref, ga_ref, halo_ref, *, tiles_per_seq):
    i, j = pl.program_id(0), pl.program_id(1)
    tm = u_ref.shape[0]

    @pl.when((i == 0) & (j == 0))
    def _():
        halo_ref[...] = jnp.zeros_like(halo_ref)

    u = u_ref[...]
    proj = lambda w_ref: jnp.dot(u, w_ref[...], preferred_element_type=F32)
    z = proj(wcc_ref) * proj(wcx_ref)
    zh = jnp.where(i % tiles_per_seq == 0, 0.0, halo_ref[j])
    halo_ref[j] = z[tm - SUBLANES:, :]

    w0, w1, w2 = cw_ref[0:1, :], cw_ref[1:2, :], cw_ref[2:3, :]
    conv = cbias_ref[...] + w2 * z + w1 * pltpu.roll(z, 1, axis=0) + w0 * pltpu.roll(z, 2, axis=0)
    top = lax.broadcasted_iota(jnp.int32, (SUBLANES, 1), 0)
    zt = z[:SUBLANES, :]
    conv_top = (cbias_ref[...] + w2 * zt
                + w1 * jnp.where(top < 1, pltpu.roll(zh, 1, axis=0), pltpu.roll(zt, 1, axis=0))
                + w0 * jnp.where(top < 2, pltpu.roll(zh, 2, axis=0), pltpu.roll(zt, 2, axis=0)))

    gated = jax.nn.sigmoid(proj(wgb_ref) + gbias_ref[1:2, :]) * proj(wcb_ref)
    yc_ref[...] = (gated * conv).astype(BF16)
    yc_ref[0:SUBLANES, :] = (gated[:SUBLANES, :] * conv_top).astype(BF16)
    ga_ref[...] = jax.nn.sigmoid(proj(wga_ref) + gbias_ref[0:1, :]).astype(BF16)


def _conv_gate(u, w, conv_w, conv_b, gate_bias, *, seq, tm, tn):
    n, d = u.shape
    width = w.shape[1] // 5
    nj = width // tn
    wspec = lambda g: pl.BlockSpec((d, tn), lambda i, j: (0, g * nj + j))
    cols = lambda r: pl.BlockSpec((r, tn), lambda i, j: (0, j))
    out = pl.BlockSpec((tm, tn), lambda i, j: (i, j))
    shape = jax.ShapeDtypeStruct((n, width), BF16)
    return pl.pallas_call(
        functools.partial(_conv_gate_kernel, tiles_per_seq=seq // tm),
        grid=(n // tm, nj),
        in_specs=[pl.BlockSpec((tm, d), lambda i, j: (i, 0)),
                  wspec(0), wspec(1), wspec(2), wspec(3), wspec(4),
                  cols(CONV_TAPS), cols(1), cols(2)],
        out_specs=[out, out],
        out_shape=[shape, shape],
        scratch_shapes=[pltpu.VMEM((nj, SUBLANES, tn), F32)],
        compiler_params=_params("arbitrary", "arbitrary"),
        name="conv_gate",
    )(u, w, w, w, w, w, conv_w, conv_b, gate_bias)


def _split3(x):
    hi = x.astype(BF16)
    r = x - hi.astype(F32)
    mid = r.astype(BF16)
    lo = (r - mid.astype(F32)).astype(BF16)
    return hi, mid, lo


def _forget_kernel(f_ref, b_ref, qa_ref, ka_ref, *, blk, n_heads):
    s = f_ref.shape[0]
    x = f_ref[...] + b_ref[...]
    logf = jnp.minimum(x, 0.0) - jnp.log1p(jnp.exp(-jnp.abs(x)))

    r = lax.broadcasted_iota(jnp.int32, (blk, blk), 0)
    c = lax.broadcasted_iota(jnp.int32, (blk, blk), 1)
    tri = (r >= c).astype(BF16)

    er = lax.broadcasted_iota(jnp.int32, (LANES, LANES), 0)
    ec = lax.broadcasted_iota(jnp.int32, (LANES, LANES), 1)
    lane = lax.broadcasted_iota(jnp.int32, (1, LANES), 1)
    slot = lane % AUG_PER_HEAD
    live = lane < n_heads * AUG_PER_HEAD

    def place(k):
        return ((ec == AUG_PER_HEAD * er + k) & (er < n_heads)).astype(BF16)

    q_ones = ((slot >= 3) & (slot < 6) & live).astype(F32)
    k_ones = ((slot < 3) & live).astype(F32)

    carry = jnp.zeros((1, LANES), F32)
    for b in range(s // blk):
        parts = _split3(logf[b * blk:(b + 1) * blk])
        local = sum(jnp.dot(tri, p, preferred_element_type=F32) for p in parts)
        cum = local + carry
        carry = cum[blk - 1:blk, :]
        c3 = _split3(cum * LOG2E)
        qa = sum(jnp.dot(c3[k], place(k), preferred_element_type=F32) for k in range(3))
        ka = sum(jnp.dot(c3[k], place(k + 3), preferred_element_type=F32) for k in range(3))
        qa_ref[b * blk:(b + 1) * blk, :] = (qa + q_ones).astype(BF16)
        ka_ref[b * blk:(b + 1) * blk, :] = (k_ones - ka).astype(BF16)


def _forget_aug(f, bias_row, *, batch, seq, n_heads):
    body = functools.partial(_forget_kernel, blk=256, n_heads=n_heads)
    spec = pl.BlockSpec((seq, LANES), lambda b: (b, 0))
    shape = jax.ShapeDtypeStruct((batch * seq, LANES), BF16)
    return pl.pallas_call(
        body,
        grid=(batch,),
        in_specs=[spec, pl.BlockSpec((1, LANES), lambda b: (0, 0))],
        out_specs=[spec, spec],
        out_shape=[shape, shape],
        compiler_params=_params("parallel"),
        name="forget_aug",
    )(f, bias_row)


def _attn_kernel(q_ref, k_ref, v_ref, qa_ref, ka_ref, o_ref, qc_ref, kc_ref, vc_ref, *, tq):
    hp, s, dh = q_ref.shape
    heads = range(hp)
    lane = lax.broadcasted_iota(jnp.int32, (1, LANES), 1)
    ones_col = jnp.broadcast_to((lane == 0).astype(BF16), (s, LANES))
    for a in heads:
        mine = (lane // AUG_PER_HEAD) == pl.program_id(1) * hp + a
        qc_ref[a, :, :dh] = q_ref[a]
        qc_ref[a, :, dh:] = qa_ref[...]
        kc_ref[a, :, :dh] = k_ref[a]
        kc_ref[a, :, dh:] = jnp.where(mine, ka_ref[...], jnp.zeros_like(ka_ref[...]))
        vc_ref[a, :, :dh] = v_ref[a]
        vc_ref[a, :, dh:] = ones_col

    row = lax.broadcasted_iota(jnp.int32, (tq, tq), 0)
    col = lax.broadcasted_iota(jnp.int32, (tq, tq), 1)
    causal = col <= row
    qk = lambda x, y: lax.dot_general(x, y, NT_DIMS, preferred_element_type=F32)
    pv = lambda x, y: jnp.dot(x.astype(BF16), y, preferred_element_type=F32)
    rowmax = lambda x: jnp.max(x, axis=-1, keepdims=True)

    for i in reversed(range(s // tq)):
        lo = i * tq
        qi = [qc_ref[a, lo:lo + tq, :] for a in heads]
        sd = [jnp.where(causal, qk(qi[a], kc_ref[a, lo:lo + tq, :]), NEG_BIG) for a in heads]
        m = [rowmax(x) for x in sd]
        if i > 0:
            so = [qk(qi[a], kc_ref[a, 0:lo, :]) for a in heads]
            m = [jnp.maximum(m[a], rowmax(so[a])) for a in heads]
        acc = [pv(jnp.exp2(sd[a] - m[a]), vc_ref[a, lo:lo + tq, :]) for a in heads]
        if i > 0:
            acc = [acc[a] + pv(jnp.exp2(so[a] - m[a]), vc_ref[a, 0:lo, :]) for a in heads]
        for a in heads:
            o_ref[a, lo:lo + tq, :] = (acc[a][:, :dh] / acc[a][:, dh:dh + 1]).astype(o_ref.dtype)


def _fox_attention(qkv, qa, ka, *, n_heads, tq, hp):
    batch, _, seq, dh = qkv.shape
    groups = n_heads // hp
    blk = lambda off: pl.BlockSpec((None, hp, seq, dh), lambda b, g: (b, off + g, 0, 0))
    aug = pl.BlockSpec((seq, LANES), lambda b, g: (b, 0))
    return pl.pallas_call(
        functools.partial(_attn_kernel, tq=tq),
        grid=(batch, groups),
        in_specs=[blk(0), blk(groups), blk(2 * groups), aug, aug],
        out_specs=blk(0),
        out_shape=jax.ShapeDtypeStruct((batch, n_heads, seq, dh), BF16),
        scratch_shapes=[pltpu.VMEM((hp, seq, dh + LANES), BF16)] * 3,
        compiler_params=_params("parallel", "arbitrary"),
        name="fox_attn",
    )(qkv, qkv, qkv, qa, ka)


def _mix_out_kernel(h_ref, y_ref, ga_ref, yc_ref, wout_ref, gmix_ref, gq_ref, wmq_ref,
                    km_ref, vm_ref, wmo_ref, gmem_ref, gnext_ref, o_ref, un_ref, *, sub):
    n_heads, tm, dh = y_ref.shape
    tiles = [slice(t * sub, (t + 1) * sub) for t in range(tm // sub)]
    mm = lambda x, w_ref: jnp.dot(x, w_ref[...], preferred_element_type=F32)

    def merge(rs):
        parts = []
        for hd in range(n_heads):
            sl = slice(hd * dh, (hd + 1) * dh)
            part = (ga_ref[rs, sl].astype(F32) * y_ref[hd, rs, :].astype(F32)
                    + yc_ref[rs, sl].astype(F32))
            parts.append(part.astype(BF16))
        return jnp.concatenate(parts, axis=-1)

    def mem_attention(qm):
        heads = []
        for hd in range(wmq_ref.shape[1] // MEM_HEAD_DIM):
            sl = slice(hd * MEM_HEAD_DIM, (hd + 1) * MEM_HEAD_DIM)
            s = lax.dot_general(qm[:, sl], km_ref[:, sl], NT_DIMS, preferred_element_type=F32)
            p = jnp.exp2(s - jnp.max(s, axis=-1, keepdims=True))
            l = jnp.sum(p, axis=-1, keepdims=True)
            o = jnp.dot(p.astype(BF16), vm_ref[:, sl], preferred_element_type=F32)
            heads.append((o / l).astype(BF16))
        return jnp.concatenate(heads, axis=-1)

    r = [mm(merge(rs), wout_ref) for rs in tiles]
    h2 = [h_ref[rs, :] + _rms(x, gmix_ref[...]) for rs, x in zip(tiles, r)]
    um = [_rms(x, gq_ref[...]).astype(BF16) for x in h2]
    qm = [(mm(x, wmq_ref) * (MEM_HEAD_DIM ** -0.5 * LOG2E)).astype(BF16) for x in um]
    om = [mem_attention(x) for x in qm]
    mo = [mm(x, wmo_ref) for x in om]
    h3 = [a + _rms(b, gmem_ref[...]) for a, b in zip(h2, mo)]
    for rs, x in zip(tiles, h3):
        o_ref[rs, :] = x
        un_ref[rs, :] = _rms(x, gnext_ref[...]).astype(BF16)


def _mix_out(h, y, ga, yc, kv, w_out, g_mix, g_q, w_mq, w_mo, g_mem, g_next, *, seq, tm, sub):
    n, d = h.shape
    n_heads, dh = y.shape[1], y.shape[3]
    n_mem = kv.shape[0] // (n // seq)
    mw = w_mq.shape[1]
    tiles_per_seq = seq // tm
    rows = pl.BlockSpec((tm, d), lambda i: (i, 0))
    return pl.pallas_call(
        functools.partial(_mix_out_kernel, sub=sub),
        grid=(n // tm,),
        in_specs=[
            rows,
            pl.BlockSpec((None, n_heads, tm, dh),
                         lambda i: (i // tiles_per_seq, 0, i % tiles_per_seq, 0)),
            rows, rows,
            _resident((d, d)), _resident((1, d)),
            _resident((1, d)), _resident((d, mw)),
            pl.BlockSpec((n_mem, mw), lambda i: (i // tiles_per_seq, 0)),
            pl.BlockSpec((n_mem, mw), lambda i: (i // tiles_per_seq, 1)),
            _resident((mw, d)), _resident((1, d)), _resident((1, d)),
        ],
        out_specs=[rows, rows],
        out_shape=[jax.ShapeDtypeStruct((n, d), F32), jax.ShapeDtypeStruct((n, d), BF16)],
        compiler_params=_params("parallel"),
        name="mix_out",
    )(h, y, ga, yc, w_out, g_mix, g_q, w_mq, kv, kv, w_mo, g_mem, g_next)


def _mem_kv_kernel(x_ref, g_ref, w_ref, o_ref):
    xn = _rms(x_ref[...], g_ref[...]).astype(BF16)
    o_ref[...] = jnp.dot(xn, w_ref[...], preferred_element_type=F32).astype(BF16)


def _mem_kv(mem, g, w, *, tm):
    n, d = mem.shape
    cols = w.shape[1]
    return pl.pallas_call(
        _mem_kv_kernel,
        grid=(n // tm,),
        in_specs=[pl.BlockSpec((tm, d), lambda i: (i, 0)), _resident((1, d)),
                  _resident((d, cols))],
        out_specs=pl.BlockSpec((tm, cols), lambda i: (i, 0)),
        out_shape=jax.ShapeDtypeStruct((n, cols), BF16),
        compiler_params=_params("parallel"),
        name="mem_kv",
    )(mem, g, w)


def kernel(x, mem, ffn1_pre_g, ffn1_w13, ffn1_w2, ffn1_post_g, mix_pre_g, w_in, forget_bias,
           gate_bias, conv_w, conv_b, w_out, mix_post_g, mem_q_pre_g, mem_kv_g, w_mq, w_mkv,
           w_mo, mem_post_g, ffn2_pre_g, ffn2_w13, ffn2_w2, ffn2_post_g):
    batch, seq, d = x.shape
    n_mem = mem.shape[1]
    depth = ffn1_w13.shape[0]
    n_heads = forget_bias.shape[1]
    fox_w = n_heads * FOX_HEAD_DIM
    assert n_heads * AUG_PER_HEAD <= LANES and fox_w == d and conv_w.shape[2] == d

    row = lambda v: v.reshape(1, -1).astype(F32)
    h = x.reshape(batch * seq, d)
    mem2 = mem.reshape(batch * n_mem, d)
    for l in range(depth):
        h, u = _ffn(h, ffn1_w13[l].astype(BF16), ffn1_w2[l].astype(BF16),
                    0.5 * row(ffn1_post_g[l]), g_pre=row(ffn1_pre_g[l]),
                    g_next=row(mix_pre_g[l]), tm=512, tf=512)

        w = w_in[l]
        w_qkv = w[:, :3 * fox_w].astype(BF16)
        w_f = jnp.pad(w[:, 3 * fox_w:3 * fox_w + n_heads].astype(BF16),
                      ((0, 0), (0, LANES - n_heads)))
        w_cg = w[:, 3 * fox_w + n_heads:].astype(BF16)
        qkv, f = _qkv_proj(u, w_qkv, w_f, batch=batch, seq=seq, tm=1024, tn=2048,
                           q_scale=FOX_HEAD_DIM ** -0.5 * LOG2E)
        yc, ga = _conv_gate(u, w_cg, conv_w[l], row(conv_b[l]),
                            gate_bias[l], seq=seq, tm=1024, tn=256)
        fb = jnp.pad(forget_bias[l], (0, LANES - n_heads)).reshape(1, LANES)
        qa, ka = _forget_aug(f, fb, batch=batch, seq=seq, n_heads=n_heads)
        y = _fox_attention(qkv, qa, ka, n_heads=n_heads, tq=256, hp=4)

        kv = _mem_kv(mem2, row(mem_kv_g[l]), w_mkv[l].astype(BF16), tm=1024)
        h, u = _mix_out(h, y, ga, yc, kv, w_out[l].astype(BF16), row(mix_post_g[l]),
                        row(mem_q_pre_g[l]), w_mq[l].astype(BF16), w_mo[l].astype(BF16),
                        row(mem_post_g[l]), row(ffn2_pre_g[l]), seq=seq, tm=512, sub=256)

        h = _ffn(h, ffn2_w13[l].astype(BF16), ffn2_w2[l].astype(BF16),
                 0.5 * row(ffn2_post_g[l]), xn=u, tm=512, tf=512)
    return h.reshape(batch, seq, d)
```

```python
import functools
import math

import jax
import jax.numpy as jnp
from jax import lax
from jax.experimental import pallas as pl
from jax.experimental.pallas import tpu as pltpu

EPS = 1e-6
FOX_HEAD_DIM = 128
MEM_HEAD_DIM = 128
CONV_TAPS = 3

LANES = 128
SUBLANES = 8
VMEM_LIMIT_BYTES = 56 * 1024 * 1024
AUG_PER_HEAD = 8
NEG_BIG = -1e30
LOG2E = math.log2(math.e)

BF16 = jnp.bfloat16
F32 = jnp.float32
NT_DIMS = (((1,), (1,)), ((), ()))


def _rms(x, g):
    ms = jnp.mean(x * x, axis=-1, keepdims=True)
    return x * lax.rsqrt(ms + EPS) * g


def _params(*sem):
    return pltpu.CompilerParams(dimension_semantics=sem, vmem_limit_bytes=VMEM_LIMIT_BYTES)


def _resident(shape):
    return pl.BlockSpec(shape, lambda *_: (0,) * len(shape), pipeline_mode=pl.Buffered(1))


def _ffn_kernel(*refs, has_xn, emit_next, n_pairs, single_at):
    refs = list(refs)
    x_ref = refs.pop(0)
    xin_ref = refs.pop(0) if has_xn else None
    gpre_ref = None if has_xn else refs.pop(0)
    w1a_ref, w3a_ref, w2a_ref, w1b_ref, w3b_ref, w2b_ref, gpost_ref = refs[:7]
    refs = refs[7:]
    gnext_ref = refs.pop(0) if emit_next else None
    o_ref = refs.pop(0)
    un_ref = refs.pop(0) if emit_next else None
    xn_ref = xin_ref if has_xn else refs.pop(0)
    j = pl.program_id(1)

    def chunk(xn, w1_ref, w3_ref, w2_ref):
        gate = jnp.dot(xn, w1_ref[...], preferred_element_type=F32)
        up = jnp.dot(xn, w3_ref[...], preferred_element_type=F32)
        act = (gate * jax.nn.sigmoid(gate) * up).astype(BF16)
        return jnp.dot(act, w2_ref[...], preferred_element_type=F32)

    def pair(xn):
        return chunk(xn, w1a_ref, w3a_ref, w2a_ref) + chunk(xn, w1b_ref, w3b_ref, w2b_ref)

    @pl.when(j == 0)
    def _():
        if has_xn:
            xn = xn_ref[...]
        else:
            xn = _rms(x_ref[...], gpre_ref[...]).astype(BF16)
            xn_ref[...] = xn
        o_ref[...] = pair(xn)

    @pl.when((j > 0) & (j < n_pairs) & (j != single_at))
    def _():
        o_ref[...] += pair(xn_ref[...])

    @pl.when(j == single_at)
    def _():
        o_ref[...] += chunk(xn_ref[...], w1a_ref, w3a_ref, w2a_ref)

    @pl.when(j == n_pairs)
    def _():
        f = o_ref[...] + pair(xn_ref[...])
        h = x_ref[...] + _rms(f, gpost_ref[...])
        o_ref[...] = h
        if emit_next:
            un_ref[...] = _rms(h, gnext_ref[...]).astype(BF16)


def _ffn(x, w13, w2, g_post, *, g_pre=None, xn=None, g_next=None, tm, tf):
    n, d = x.shape
    d_ff = w2.shape[0]
    nf = d_ff // tf
    assert nf % 2 == 1 and nf >= 5, "d_ff chunks are walked as pairs plus one single"
    n_pairs = nf // 2
    single_at = n_pairs // 2
    chunk_a = lambda j: jnp.where(j <= single_at, 2 * j, 2 * j - 1)
    chunk_b = lambda j: jnp.where(j < single_at, 2 * j + 1,
                                  jnp.where(j == single_at, 2 * j + 2, 2 * j))
    has_xn, emit_next = xn is not None, g_next is not None
    rows = pl.BlockSpec((tm, d), lambda i, j: (i, 0))
    vec = pl.BlockSpec((1, d), lambda i, j: (0, 0))
    args, in_specs = [x], [rows]
    if has_xn:
        args.append(xn); in_specs.append(rows)
    else:
        args.append(g_pre); in_specs.append(vec)
    for chunk in (chunk_a, chunk_b):
        args += [w13, w13, w2]
        in_specs += [
            pl.BlockSpec((d, tf), lambda i, j, c=chunk: (0, c(j))),
            pl.BlockSpec((d, tf), lambda i, j, c=chunk: (0, c(j) + nf)),
            pl.BlockSpec((tf, d), lambda i, j, c=chunk: (c(j), 0)),
        ]
    args.append(g_post); in_specs.append(vec)
    out_specs, out_shape = rows, jax.ShapeDtypeStruct((n, d), F32)
    if emit_next:
        args.append(g_next); in_specs.append(vec)
        out_specs = [out_specs, rows]
        out_shape = [out_shape, jax.ShapeDtypeStruct((n, d), BF16)]
    body = functools.partial(_ffn_kernel, has_xn=has_xn, emit_next=emit_next, n_pairs=n_pairs,
                             single_at=single_at)
    return pl.pallas_call(
        body,
        grid=(n // tm, n_pairs + 1),
        in_specs=in_specs,
        out_specs=out_specs,
        out_shape=out_shape,
        scratch_shapes=[] if has_xn else [pltpu.VMEM((tm, d), BF16)],
        compiler_params=_params("parallel", "arbitrary"),
        name="ffn",
    )(*args)


def _qkv_kernel(u_ref, w_ref, wf_ref, o_ref, f_ref, *, n_scaled, scale):
    j = pl.program_id(1)

    @pl.when(j == 0)
    def _():
        f_ref[...] = jnp.dot(u_ref[...], wf_ref[...], preferred_element_type=F32)

    acc = jnp.dot(u_ref[...], w_ref[...], preferred_element_type=F32)
    acc = acc * jnp.where(j < n_scaled, scale, 1.0).astype(F32)
    for hh in range(o_ref.shape[0]):
        o_ref[hh] = acc[:, hh * LANES:(hh + 1) * LANES].astype(BF16)


def _qkv_proj(u, w, w_f, *, batch, seq, tm, tn, q_scale):
    n, d = u.shape
    cols = w.shape[1]
    tiles_per_seq = seq // tm
    heads_per_blk = tn // FOX_HEAD_DIM
    body = functools.partial(_qkv_kernel, n_scaled=cols // 3 // tn, scale=q_scale)
    return pl.pallas_call(
        body,
        grid=(n // tm, cols // tn),
        in_specs=[
            pl.BlockSpec((tm, d), lambda i, j: (i, 0)),
            pl.BlockSpec((d, tn), lambda i, j: (0, j)),
            pl.BlockSpec((d, LANES), lambda i, j: (0, 0)),
        ],
        out_specs=[
            pl.BlockSpec((None, heads_per_blk, tm, FOX_HEAD_DIM),
                         lambda i, j: (i // tiles_per_seq, j, i % tiles_per_seq, 0)),
            pl.BlockSpec((tm, LANES), lambda i, j: (i, 0)),
        ],
        out_shape=[
            jax.ShapeDtypeStruct((batch, cols // FOX_HEAD_DIM, seq, FOX_HEAD_DIM), BF16),
            jax.ShapeDtypeStruct((n, LANES), F32),
        ],
        compiler_params=_params("parallel", "arbitrary"),
        name="qkv_proj",
    )(u, w, w_f)


def _conv_gate_kernel(u_ref, wcx_ref, wcb_ref, wcc_ref, wga_ref, wgb_ref, cw_ref, cbias_ref,
                      gbias_ref, yc_ref, ga_ref, halo_ref, *, tiles_per_seq):
    i, j = pl.program_id(0), pl.program_id(1)
    tm = u_ref.shape[0]

    @pl.when((i == 0) & (j == 0))
    def _():
        halo_ref[...] = jnp.zeros_like(halo_ref)

    u = u_ref[...]
    proj = lambda w_ref: jnp.dot(u, w_ref[...], preferred_element_type=F32)
    z = proj(wcc_ref) * proj(wcx_ref)
    zh = jnp.where(i % tiles_per_seq == 0, 0.0, halo_ref[j])
    halo_ref[j] = z[tm - SUBLANES:, :]

    w0, w1, w2 = cw_ref[0:1, :], cw_ref[1:2, :], cw_ref[2:3, :]
    conv = cbias_ref[...] + w2 * z + w1 * pltpu.roll(z, 1, axis=0) + w0 * pltpu.roll(z, 2, axis=0)
    top = lax.broadcasted_iota(jnp.int32, (SUBLANES, 1), 0)
    zt = z[:SUBLANES, :]
    conv_top = (cbias_ref[...] + w2 * zt
                + w1 * jnp.where(top < 1, pltpu.roll(zh, 1, axis=0), pltpu.roll(zt, 1, axis=0))
                + w0 * jnp.where(top < 2, pltpu.roll(zh, 2, axis=0), pltpu.roll(zt, 2, axis=0)))

    gated = jax.nn.sigmoid(proj(wgb_ref) + gbias_ref[1:2, :]) * proj(wcb_ref)
    yc_ref[...] = (gated * conv).astype(BF16)
    yc_ref[0:SUBLANES, :] = (gated[:SUBLANES, :] * conv_top).astype(BF16)
    ga_ref[...] = jax.nn.sigmoid(proj(wga_ref) + gbias_ref[0:1, :]).astype(BF16)


def _conv_gate(u, w, conv_w, conv_b, gate_bias, *, seq, tm, tn):
    n, d = u.shape
    width = w.shape[1] // 5
    nj = width // tn
    wspec = lambda g: pl.BlockSpec((d, tn), lambda i, j: (0, g * nj + j))
    cols = lambda r: pl.BlockSpec((r, tn), lambda i, j: (0, j))
    out = pl.BlockSpec((tm, tn), lambda i, j: (i, j))
    shape = jax.ShapeDtypeStruct((n, width), BF16)
    return pl.pallas_call(
        functools.partial(_conv_gate_kernel, tiles_per_seq=seq // tm),
        grid=(n // tm, nj),
        in_specs=[pl.BlockSpec((tm, d), lambda i, j: (i, 0)),
                  wspec(0), wspec(1), wspec(2), wspec(3), wspec(4),
                  cols(CONV_TAPS), cols(1), cols(2)],
        out_specs=[out, out],
        out_shape=[shape, shape],
        scratch_shapes=[pltpu.VMEM((nj, SUBLANES, tn), F32)],
        compiler_params=_params("arbitrary", "arbitrary"),
        name="conv_gate",
    )(u, w, w, w, w, w, conv_w, conv_b, gate_bias)


def _split3(x):
    hi = x.astype(BF16)
    r = x - hi.astype(F32)
    mid = r.astype(BF16)
    lo = (r - mid.astype(F32)).astype(BF16)
    return hi, mid, lo


def _forget_kernel(f_ref, b_ref, qa_ref, ka_ref, *, blk, n_heads):
    s = f_ref.shape[0]
    x = f_ref[...] + b_ref[...]
    logf = jnp.minimum(x, 0.0) - jnp.log1p(jnp.exp(-jnp.abs(x)))

    r = lax.broadcasted_iota(jnp.int32, (blk, blk), 0)
    c = lax.broadcasted_iota(jnp.int32, (blk, blk), 1)
    tri = (r >= c).astype(BF16)

    er = lax.broadcasted_iota(jnp.int32, (LANES, LANES), 0)
    ec = lax.broadcasted_iota(jnp.int32, (LANES, LANES), 1)
    lane = lax.broadcasted_iota(jnp.int32, (1, LANES), 1)
    slot = lane % AUG_PER_HEAD
    live = lane < n_heads * AUG_PER_HEAD

    def place(k):
        return ((ec == AUG_PER_HEAD * er + k) & (er < n_heads)).astype(BF16)

    q_ones = ((slot >= 3) & (slot < 6) & live).astype(F32)
    k_ones = ((slot < 3) & live).astype(F32)

    carry = jnp.zeros((1, LANES), F32)
    for b in range(s // blk):
        parts = _split3(logf[b * blk:(b + 1) * blk])
        local = sum(jnp.dot(tri, p, preferred_element_type=F32) for p in parts)
        cum = local + carry
        carry = cum[blk - 1:blk, :]
        c3 = _split3(cum * LOG2E)
        qa = sum(jnp.dot(c3[k], place(k), preferred_element_type=F32) for k in range(3))
        ka = sum(jnp.dot(c3[k], place(k + 3), preferred_element_type=F32) for k in range(3))
        qa_ref[b * blk:(b + 1) * blk, :] = (qa + q_ones).astype(BF16)
        ka_ref[b * blk:(b + 1) * blk, :] = (k_ones - ka).astype(BF16)


def _forget_aug(f, bias_row, *, batch, seq, n_heads):
    body = functools.partial(_forget_kernel, blk=256, n_heads=n_heads)
    spec = pl.BlockSpec((seq, LANES), lambda b: (b, 0))
    shape = jax.ShapeDtypeStruct((batch * seq, LANES), BF16)
    return pl.pallas_call(
        body,
        grid=(batch,),
        in_specs=[spec, pl.BlockSpec((1, LANES), lambda b: (0, 0))],
        out_specs=[spec, spec],
        out_shape=[shape, shape],
        compiler_params=_params("parallel"),
        name="forget_aug",
    )(f, bias_row)


def _attn_kernel(q_ref, k_ref, v_ref, qa_ref, ka_ref, o_ref, qc_ref, kc_ref, vc_ref, *, tq):
    hp, s, dh = q_ref.shape
    heads = range(hp)
    lane = lax.broadcasted_iota(jnp.int32, (1, LANES), 1)
    ones_col = jnp.broadcast_to((lane == 0).astype(BF16), (s, LANES))
    for a in heads:
        mine = (lane // AUG_PER_HEAD) == pl.program_id(1) * hp + a
        qc_ref[a, :, :dh] = q_ref[a]
        qc_ref[a, :, dh:] = qa_ref[...]
        kc_ref[a, :, :dh] = k_ref[a]
        kc_ref[a, :, dh:] = jnp.where(mine, ka_ref[...], jnp.zeros_like(ka_ref[...]))
        vc_ref[a, :, :dh] = v_ref[a]
        vc_ref[a, :, dh:] = ones_col

    row = lax.broadcasted_iota(jnp.int32, (tq, tq), 0)
    col = lax.broadcasted_iota(jnp.int32, (tq, tq), 1)
    causal = col <= row
    qk = lambda x, y: lax.dot_general(x, y, NT_DIMS, preferred_element_type=F32)
    pv = lambda x, y: jnp.dot(x.astype(BF16), y, preferred_element_type=F32)
    rowmax = lambda x: jnp.max(x, axis=-1, keepdims=True)

    for i in reversed(range(s // tq)):
        lo = i * tq
        qi = [qc_ref[a, lo:lo + tq, :] for a in heads]
        sd = [jnp.where(causal, qk(qi[a], kc_ref[a, lo:lo + tq, :]), NEG_BIG) for a in heads]
        m = [rowmax(x) for x in sd]
        if i > 0:
            so = [qk(qi[a], kc_ref[a, 0:lo, :]) for a in heads]
            m = [jnp.maximum(m[a], rowmax(so[a])) for a in heads]
        acc = [pv(jnp.exp2(sd[a] - m[a]), vc_ref[a, lo:lo + tq, :]) for a in heads]
        if i > 0:
            acc = [acc[a] + pv(jnp.exp2(so[a] - m[a]), vc_ref[a, 0:lo, :]) for a in heads]
        for a in heads:
            o_ref[a, lo:lo + tq, :] = (acc[a][:, :dh] / acc[a][:, dh:dh + 1]).astype(o_ref.dtype)


def _fox_attention(qkv, qa, ka, *, n_heads, tq, hp):
    batch, _, seq, dh = qkv.shape
    groups = n_heads // hp
    blk = lambda off: pl.BlockSpec((None, hp, seq, dh), lambda b, g: (b, off + g, 0, 0))
    aug = pl.BlockSpec((seq, LANES), lambda b, g: (b, 0))
    return pl.pallas_call(
        functools.partial(_attn_kernel, tq=tq),
        grid=(batch, groups),
        in_specs=[blk(0), blk(groups), blk(2 * groups), aug, aug],
        out_specs=blk(0),
        out_shape=jax.ShapeDtypeStruct((batch, n_heads, seq, dh), BF16),
        scratch_shapes=[pltpu.VMEM((hp, seq, dh + LANES), BF16)] * 3,
        compiler_params=_params("parallel", "arbitrary"),
        name="fox_attn",
    )(qkv, qkv, qkv, qa, ka)


def _mix_out_kernel(h_ref, y_ref, ga_ref, yc_ref, wout_ref, gmix_ref, gq_ref, wmq_ref,
                    km_ref, vm_ref, wmo_ref, gmem_ref, gnext_ref, o_ref, un_ref, *, sub):
    n_heads, tm, dh = y_ref.shape
    tiles = [slice(t * sub, (t + 1) * sub) for t in range(tm // sub)]
    mm = lambda x, w_ref: jnp.dot(x, w_ref[...], preferred_element_type=F32)

    def merge(rs):
        parts = []
        for hd in range(n_heads):
            sl = slice(hd * dh, (hd + 1) * dh)
            part = (ga_ref[rs, sl].astype(F32) * y_ref[hd, rs, :].astype(F32)
                    + yc_ref[rs, sl].astype(F32))
            parts.append(part.astype(BF16))
        return jnp.concatenate(parts, axis=-1)

    def mem_attention(qm):
        heads = []
        for hd in range(wmq_ref.shape[1] // MEM_HEAD_DIM):
            sl = slice(hd * MEM_HEAD_DIM, (hd + 1) * MEM_HEAD_DIM)
            s = lax.dot_general(qm[:, sl], km_ref[:, sl], NT_DIMS, preferred_element_type=F32)
            p = jnp.exp2(s - jnp.max(s, axis=-1, keepdims=True))
            l = jnp.sum(p, axis=-1, keepdims=True)
            o = jnp.dot(p.astype(BF16), vm_ref[:, sl], preferred_element_type=F32)
            heads.append((o / l).astype(BF16))
        return jnp.concatenate(heads, axis=-1)

    r = [mm(merge(rs), wout_ref) for rs in tiles]
    h2 = [h_ref[rs, :] + _rms(x, gmix_ref[...]) for rs, x in zip(tiles, r)]
    um = [_rms(x, gq_ref[...]).astype(BF16) for x in h2]
    qm = [(mm(x, wmq_ref) * (MEM_HEAD_DIM ** -0.5 * LOG2E)).astype(BF16) for x in um]
    om = [mem_attention(x) for x in qm]
    mo = [mm(x, wmo_ref) for x in om]
    h3 = [a + _rms(b, gmem_ref[...]) for a, b in zip(h2, mo)]
    for rs, x in zip(tiles, h3):
        o_ref[rs, :] = x
        un_ref[rs, :] = _rms(x, gnext_ref[...]).astype(BF16)


def _mix_out(h, y, ga, yc, kv, w_out, g_mix, g_q, w_mq, w_mo, g_mem, g_next, *, seq, tm, sub):
    n, d = h.shape
    n_heads, dh = y.shape[1], y.shape[3]
    n_mem = kv.shape[0] // (n // seq)
    mw = w_mq.shape[1]
    tiles_per_seq = seq // tm
    rows = pl.BlockSpec((tm, d), lambda i: (i, 0))
    return pl.pallas_call(
        functools.partial(_mix_out_kernel, sub=sub),
        grid=(n // tm,),
        in_specs=[
            rows,
            pl.BlockSpec((None, n_heads, tm, dh),
                         lambda i: (i // tiles_per_seq, 0, i % tiles_per_seq, 0)),
            rows, rows,
            _resident((d, d)), _resident((1, d)),
            _resident((1, d)), _resident((d, mw)),
            pl.BlockSpec((n_mem, mw), lambda i: (i // tiles_per_seq, 0)),
            pl.BlockSpec((n_mem, mw), lambda i: (i // tiles_per_seq, 1)),
            _resident((mw, d)), _resident((1, d)), _resident((1, d)),
        ],
        out_specs=[rows, rows],
        out_shape=[jax.ShapeDtypeStruct((n, d), F32), jax.ShapeDtypeStruct((n, d), BF16)],
        compiler_params=_params("parallel"),
        name="mix_out",
    )(h, y, ga, yc, w_out, g_mix, g_q, w_mq, kv, kv, w_mo, g_mem, g_next)


def _mem_kv_kernel(x_ref, g_ref, w_ref, o_ref):
    xn = _rms(x_ref[...], g_ref[...]).astype(BF16)
    o_ref[...] = jnp.dot(xn, w_ref[...], preferred_element_type=F32).astype(BF16)


def _mem_kv(mem, g, w, *, tm):
    n, d = mem.shape
    cols = w.shape[1]
    return pl.pallas_call(
        _mem_kv_kernel,
        grid=(n // tm,),
        in_specs=[pl.BlockSpec((tm, d), lambda i: (i, 0)), _resident((1, d)),
                  _resident((d, cols))],
        out_specs=pl.BlockSpec((tm, cols), lambda i: (i, 0)),
        out_shape=jax.ShapeDtypeStruct((n, cols), BF16),
        compiler_params=_params("parallel"),
        name="mem_kv",
    )(mem, g, w)


def kernel(x, mem, ffn1_pre_g, ffn1_w13, ffn1_w2, ffn1_post_g, mix_pre_g, w_in, forget_bias,
           gate_bias, conv_w, conv_b, w_out, mix_post_g, mem_q_pre_g, mem_kv_g, w_mq, w_mkv,
           w_mo, mem_post_g, ffn2_pre_g, ffn2_w13, ffn2_w2, ffn2_post_g):
    batch, seq, d = x.shape
    n_mem = mem.shape[1]
    depth = ffn1_w13.shape[0]
    n_heads = forget_bias.shape[1]
    fox_w = n_heads * FOX_HEAD_DIM
    assert n_heads * AUG_PER_HEAD <= LANES and fox_w == d and conv_w.shape[2] == d

    row = lambda v: v.reshape(1, -1).astype(F32)
    h = x.reshape(batch * seq, d)
    mem2 = mem.reshape(batch * n_mem, d)
    for l in range(depth):
        h, u = _ffn(h, ffn1_w13[l].astype(BF16), ffn1_w2[l].astype(BF16),
                    0.5 * row(ffn1_post_g[l]), g_pre=row(ffn1_pre_g[l]),
                    g_next=row(mix_pre_g[l]), tm=512, tf=512)

        w = w_in[l]
        w_qkv = w[:, :3 * fox_w].astype(BF16)
        w_f = jnp.pad(w[:, 3 * fox_w:3 * fox_w + n_heads].astype(BF16),
                      ((0, 0), (0, LANES - n_heads)))
        w_cg = w[:, 3 * fox_w + n_heads:].astype(BF16)
        qkv, f = _qkv_proj(u, w_qkv, w_f, batch=batch, seq=seq, tm=1024, tn=2048,
                           q_scale=FOX_HEAD_DIM ** -0.5 * LOG2E)
        yc, ga = _conv_gate(u, w_cg, conv_w[l], row(conv_b[l]),
                            gate_bias[l], seq=seq, tm=1024, tn=256)
        fb = jnp.pad(forget_bias[l], (0, LANES - n_heads)).reshape(1, LANES)
        qa, ka = _forget_aug(f, fb, batch=batch, seq=seq, n_heads=n_heads)
        y = _fox_attention(qkv, qa, ka, n_heads=n_heads, tq=256, hp=4)

        kv = _mem_kv(mem2, row(mem_kv_g[l]), w_mkv[l].astype(BF16), tm=1024)
        h, u = _mix_out(h, y, ga, yc, kv, w_out[l].astype(BF16), row(mix_post_g[l]),
                        row(mem_q_pre_g[l]), w_mq[l].astype(BF16), w_mo[l].astype(BF16),
                        row(mem_post_g[l]), row(ffn2_pre_g[l]), seq=seq, tm=512, sub=256)

        h = _ffn(h, ffn2_w13[l].astype(BF16), ffn2_w2[l].astype(BF16),
                 0.5 * row(ffn2_post_g[l]), xn=u, tm=512, tf=512)
    return h.reshape(batch, seq, d)
```

```python
import functools
import math

import jax
import jax.numpy as jnp
from jax import lax
from jax.experimental import pallas as pl
from jax.experimental.pallas import tpu as pltpu

EPS = 1e-6
FOX_HEAD_DIM = 128
MEM_HEAD_DIM = 128
CONV_TAPS = 3

LANES = 128
SUBLANES = 8
VMEM_LIMIT_BYTES = 56 * 1024 * 1024
AUG_PER_HEAD = 8
NEG_BIG = -1e30
LOG2E = math.log2(math.e)

BF16 = jnp.bfloat16
F32 = jnp.float32
NT_DIMS = (((1,), (1,)), ((), ()))


def _rms(x, g):
    ms = jnp.mean(x * x, axis=-1, keepdims=True)
    return x * lax.rsqrt(ms + EPS) * g


def _params(*sem):
    return pltpu.CompilerParams(dimension_semantics=sem, vmem_limit_bytes=VMEM_LIMIT_BYTES)


def _resident(shape):
    return pl.BlockSpec(shape, lambda *_: (0,) * len(shape), pipeline_mode=pl.Buffered(1))


def _ffn_kernel(*refs, has_xn, emit_next, n_pairs, single_at):
    refs = list(refs)
    x_ref = refs.pop(0)
    xin_ref = refs.pop(0) if has_xn else None
    gpre_ref = None if has_xn else refs.pop(0)
    w1a_ref, w3a_ref, w2a_ref, w1b_ref, w3b_ref, w2b_ref, gpost_ref = refs[:7]
    refs = refs[7:]
    gnext_ref = refs.pop(0) if emit_next else None
    o_ref = refs.pop(0)
    un_ref = refs.pop(0) if emit_next else None
    xn_ref = xin_ref if has_xn else refs.pop(0)
    j = pl.program_id(1)

    def chunk(xn, w1_ref, w3_ref, w2_ref):
        gate = jnp.dot(xn, w1_ref[...], preferred_element_type=F32)
        up = jnp.dot(xn, w3_ref[...], preferred_element_type=F32)
        act = (gate * jax.nn.sigmoid(gate) * up).astype(BF16)
        return jnp.dot(act, w2_ref[...], preferred_element_type=F32)

    def pair(xn):
        return chunk(xn, w1a_ref, w3a_ref, w2a_ref) + chunk(xn, w1b_ref, w3b_ref, w2b_ref)

    @pl.when(j == 0)
    def _():
        if has_xn:
            xn = xn_ref[...]
        else:
            xn = _rms(x_ref[...], gpre_ref[...]).astype(BF16)
            xn_ref[...] = xn
        o_ref[...] = pair(xn)

    @pl.when((j > 0) & (j < n_pairs) & (j != single_at))
    def _():
        o_ref[...] += pair(xn_ref[...])

    @pl.when(j == single_at)
    def _():
        o_ref[...] += chunk(xn_ref[...], w1a_ref, w3a_ref, w2a_ref)

    @pl.when(j == n_pairs)
    def _():
        f = o_ref[...] + pair(xn_ref[...])
        h = x_ref[...] + _rms(f, gpost_ref[...])
        o_ref[...] = h
        if emit_next:
            un_ref[...] = _rms(h, gnext_ref[...]).astype(BF16)


def _ffn(x, w13, w2, g_post, *, g_pre=None, xn=None, g_next=None, tm, tf):
    n, d = x.shape
    d_ff = w2.shape[0]
    nf = d_ff // tf
    assert nf % 2 == 1 and nf >= 5, "d_ff chunks are walked as pairs plus one single"
    n_pairs = nf // 2
    single_at = n_pairs // 2
    chunk_a = lambda j: jnp.where(j <= single_at, 2 * j, 2 * j - 1)
    chunk_b = lambda j: jnp.where(j < single_at, 2 * j + 1,
                                  jnp.where(j == single_at, 2 * j + 2, 2 * j))
    has_xn, emit_next = xn is not None, g_next is not None
    rows = pl.BlockSpec((tm, d), lambda i, j: (i, 0))
    vec = pl.BlockSpec((1, d), lambda i, j: (0, 0))
    args, in_specs = [x], [rows]
    if has_xn:
        args.append(xn); in_specs.append(rows)
    else:
        args.append(g_pre); in_specs.append(vec)
    for chunk in (chunk_a, chunk_b):
        args += [w13, w13, w2]
        in_specs += [
            pl.BlockSpec((d, tf), lambda i, j, c=chunk: (0, c(j))),
            pl.BlockSpec((d, tf), lambda i, j, c=chunk: (0, c(j) + nf)),
            pl.BlockSpec((tf, d), lambda i, j, c=chunk: (c(j), 0)),
        ]
    args.append(g_post); in_specs.append(vec)
    out_specs, out_shape = rows, jax.ShapeDtypeStruct((n, d), F32)
    if emit_next:
        args.append(g_next); in_specs.append(vec)
        out_specs = [out_specs, rows]
        out_shape = [out_shape, jax.ShapeDtypeStruct((n, d), BF16)]
    body = functools.partial(_ffn_kernel, has_xn=has_xn, emit_next=emit_next, n_pairs=n_pairs,
                             single_at=single_at)
    return pl.pallas_call(
        body,
        grid=(n // tm, n_pairs + 1),
        in_specs=in_specs,
        out_specs=out_specs,
        out_shape=out_shape,
        scratch_shapes=[] if has_xn else [pltpu.VMEM((tm, d), BF16)],
        compiler_params=_params("parallel", "arbitrary"),
        name="ffn",
    )(*args)


def _qkv_kernel(u_ref, w_ref, wf_ref, o_ref, f_ref, *, n_scaled, scale):
    j = pl.program_id(1)

    @pl.when(j == 0)
    def _():
        f_ref[...] = jnp.dot(u_ref[...], wf_ref[...], preferred_element_type=F32)

    acc = jnp.dot(u_ref[...], w_ref[...], preferred_element_type=F32)
    acc = acc * jnp.where(j < n_scaled, scale, 1.0).astype(F32)
    for hh in range(o_ref.shape[0]):
        o_ref[hh] = acc[:, hh * LANES:(hh + 1) * LANES].astype(BF16)


def _qkv_proj(u, w, w_f, *, batch, seq, tm, tn, q_scale):
    n, d = u.shape
    cols = w.shape[1]
    tiles_per_seq = seq // tm
    heads_per_blk = tn // FOX_HEAD_DIM
    body = functools.partial(_qkv_kernel, n_scaled=cols // 3 // tn, scale=q_scale)
    return pl.pallas_call(
        body,
        grid=(n // tm, cols // tn),
        in_specs=[
            pl.BlockSpec((tm, d), lambda i, j: (i, 0)),
            pl.BlockSpec((d, tn), lambda i, j: (0, j)),
            pl.BlockSpec((d, LANES), lambda i, j: (0, 0)),
        ],
        out_specs=[
            pl.BlockSpec((None, heads_per_blk, tm, FOX_HEAD_DIM),
                         lambda i, j: (i // tiles_per_seq, j, i % tiles_per_seq, 0)),
            pl.BlockSpec((tm, LANES), lambda i, j: (i, 0)),
        ],
        out_shape=[
            jax.ShapeDtypeStruct((batch, cols // FOX_HEAD_DIM, seq, FOX_HEAD_DIM), BF16),
            jax.ShapeDtypeStruct((n, LANES), F32),
        ],
        compiler_params=_params("parallel", "arbitrary"),
        name="qkv_proj",
    )(u, w, w_f)


def _conv_gate_kernel(u_ref, wcx_ref, wcb_ref, wcc_ref, wga_ref, wgb_ref, cw_ref, cbias_ref,
                      gbias_ref, yc_ref, ga_ref, halo_ref, *, tiles_per_seq, sub):
    i, j = pl.program_id(0), pl.program_id(1)
    tm, tn = yc_ref.shape

    @pl.when((i == 0) & (j == 0))
    def _():
        halo_ref[...] = jnp.zeros_like(halo_ref)

    u = u_ref[...]
    top = lax.broadcasted_iota(jnp.int32, (SUBLANES, 1), 0)
    for c in range(tn // sub):
        cs = slice(c * sub, (c + 1) * sub)
        proj = lambda w_ref: jnp.dot(u, w_ref[:, cs], preferred_element_type=F32)
        z = proj(wcc_ref) * proj(wcx_ref)
        zh = jnp.where(i % tiles_per_seq == 0, 0.0, halo_ref[j, :, cs])
        halo_ref[j, :, cs] = z[tm - SUBLANES:, :]

        w0, w1, w2 = cw_ref[0:1, cs], cw_ref[1:2, cs], cw_ref[2:3, cs]
        cbias = cbias_ref[:, cs]
        conv = cbias + w2 * z + w1 * pltpu.roll(z, 1, axis=0) + w0 * pltpu.roll(z, 2, axis=0)
        zt = z[:SUBLANES, :]
        conv_top = (cbias + w2 * zt
                    + w1 * jnp.where(top < 1, pltpu.roll(zh, 1, axis=0), pltpu.roll(zt, 1, axis=0))
                    + w0 * jnp.where(top < 2, pltpu.roll(zh, 2, axis=0), pltpu.roll(zt, 2, axis=0)))

        gated = jax.nn.sigmoid(proj(wgb_ref) + gbias_ref[1:2, cs]) * proj(wcb_ref)
        yc_ref[:, cs] = (gated * conv).astype(BF16)
        yc_ref[0:SUBLANES, cs] = (gated[:SUBLANES, :] * conv_top).astype(BF16)
        ga_ref[:, cs] = jax.nn.sigmoid(proj(wga_ref) + gbias_ref[0:1, cs]).astype(BF16)


def _conv_gate(u, w, conv_w, conv_b, gate_bias, *, seq, tm, tn, sub):
    n, d = u.shape
    width = w.shape[1] // 5
    nj = width // tn
    wspec = lambda g: pl.BlockSpec((d, tn), lambda i, j: (0, g * nj + j))
    cols = lambda r: pl.BlockSpec((r, tn), lambda i, j: (0, j))
    out = pl.BlockSpec((tm, tn), lambda i, j: (i, j))
    shape = jax.ShapeDtypeStruct((n, width), BF16)
    return pl.pallas_call(
        functools.partial(_conv_gate_kernel, tiles_per_seq=seq // tm, sub=sub),
        grid=(n // tm, nj),
        in_specs=[pl.BlockSpec((tm, d), lambda i, j: (i, 0)),
                  wspec(0), wspec(1), wspec(2), wspec(3), wspec(4),
                  cols(CONV_TAPS), cols(1), cols(2)],
        out_specs=[out, out],
        out_shape=[shape, shape],
        scratch_shapes=[pltpu.VMEM((nj, SUBLANES, tn), F32)],
        compiler_params=_params("arbitrary", "arbitrary"),
        name="conv_gate",
    )(u, w, w, w, w, w, conv_w, conv_b, gate_bias)


def _split3(x):
    hi = x.astype(BF16)
    r = x - hi.astype(F32)
    mid = r.astype(BF16)
    lo = (r - mid.astype(F32)).astype(BF16)
    return hi, mid, lo


def _forget_kernel(f_ref, b_ref, qa_ref, ka_ref, *, blk, n_heads):
    s = f_ref.shape[0]
    x = f_ref[...] + b_ref[...]
    logf = jnp.minimum(x, 0.0) - jnp.log1p(jnp.exp(-jnp.abs(x)))

    r = lax.broadcasted_iota(jnp.int32, (blk, blk), 0)
    c = lax.broadcasted_iota(jnp.int32, (blk, blk), 1)
    tri = (r >= c).astype(BF16)

    er = lax.broadcasted_iota(jnp.int32, (LANES, LANES), 0)
    ec = lax.broadcasted_iota(jnp.int32, (LANES, LANES), 1)
    lane = lax.broadcasted_iota(jnp.int32, (1, LANES), 1)
    slot = lane % AUG_PER_HEAD
    live = lane < n_heads * AUG_PER_HEAD

    def place(k):
        return ((ec == AUG_PER_HEAD * er + k) & (er < n_heads)).astype(BF16)

    spread = [jnp.concatenate([place(k), -place(k + 3)], axis=1) for k in range(3)]
    ones = jnp.concatenate([((slot >= 3) & (slot < 6) & live).astype(F32),
                            ((slot < 3) & live).astype(F32)], axis=1)

    carry = jnp.zeros((1, LANES), F32)
    for b in range(0, s // blk, 2):
        rows = [slice((b + t) * blk, (b + t + 1) * blk) for t in range(2)]
        parts = [_split3(logf[rs]) for rs in rows]
        local = sum(jnp.dot(tri, jnp.concatenate([parts[0][p], parts[1][p]], axis=1),
                            preferred_element_type=F32) for p in range(3))
        for t, rs in enumerate(rows):
            cum = local[:, t * LANES:(t + 1) * LANES] + carry
            carry = cum[blk - 1:blk, :]
            c3 = _split3(cum * LOG2E)
            aug = ones + sum(jnp.dot(c3[k], spread[k], preferred_element_type=F32)
                             for k in range(3))
            qa_ref[rs, :] = aug[:, :LANES].astype(BF16)
            ka_ref[rs, :] = aug[:, LANES:].astype(BF16)


def _forget_aug(f, bias_row, *, batch, seq, n_heads):
    body = functools.partial(_forget_kernel, blk=256, n_heads=n_heads)
    spec = pl.BlockSpec((seq, LANES), lambda b: (b, 0))
    shape = jax.ShapeDtypeStruct((batch * seq, LANES), BF16)
    return pl.pallas_call(
        body,
        grid=(batch,),
        in_specs=[spec, pl.BlockSpec((1, LANES), lambda b: (0, 0))],
        out_specs=[spec, spec],
        out_shape=[shape, shape],
        compiler_params=_params("parallel"),
        name="forget_aug",
    )(f, bias_row)


def _attn_kernel(q_ref, k_ref, v_ref, qa_ref, ka_ref, o_ref, qc_ref, kc_ref, vc_ref, *, tq):
    hp, s, dh = q_ref.shape
    heads = range(hp)
    lane = lax.broadcasted_iota(jnp.int32, (1, LANES), 1)
    ones_col = jnp.broadcast_to((lane == 0).astype(BF16), (s, LANES))
    for a in heads:
        mine = (lane // AUG_PER_HEAD) == pl.program_id(1) * hp + a
        qc_ref[a, :, :dh] = q_ref[a]
        qc_ref[a, :, dh:] = qa_ref[...]
        kc_ref[a, :, :dh] = k_ref[a]
        kc_ref[a, :, dh:] = jnp.where(mine, ka_ref[...], jnp.zeros_like(ka_ref[...]))
        vc_ref[a, :, :dh] = v_ref[a]
        vc_ref[a, :, dh:] = ones_col

    row = lax.broadcasted_iota(jnp.int32, (tq, tq), 0)
    col = lax.broadcasted_iota(jnp.int32, (tq, tq), 1)
    causal = col <= row
    qk = lambda x, y: lax.dot_general(x, y, NT_DIMS, preferred_element_type=F32)
    pv = lambda x, y: jnp.dot(x.astype(BF16), y, preferred_element_type=F32)
    rowmax = lambda x: jnp.max(x, axis=-1, keepdims=True)

    for i in reversed(range(s // tq)):
        lo = i * tq
        qi = [qc_ref[a, lo:lo + tq, :] for a in heads]
        sd = [jnp.where(causal, qk(qi[a], kc_ref[a, lo:lo + tq, :]), NEG_BIG) for a in heads]
        m = [rowmax(x) for x in sd]
        if i > 0:
            so = [qk(qi[a], kc_ref[a, 0:lo, :]) for a in heads]
            m = [jnp.maximum(m[a], rowmax(so[a])) for a in heads]
        acc = [pv(jnp.exp2(sd[a] - m[a]), vc_ref[a, lo:lo + tq, :]) for a in heads]
        if i > 0:
            acc = [acc[a] + pv(jnp.exp2(so[a] - m[a]), vc_ref[a, 0:lo, :]) for a in heads]
        for a in heads:
            o_ref[a, lo:lo + tq, :] = (acc[a][:, :dh] / acc[a][:, dh:dh + 1]).astype(o_ref.dtype)


def _fox_attention(qkv, qa, ka, *, n_heads, tq, hp):
    batch, _, seq, dh = qkv.shape
    groups = n_heads // hp
    blk = lambda off: pl.BlockSpec((None, hp, seq, dh), lambda b, g: (b, off + g, 0, 0))
    aug = pl.BlockSpec((seq, LANES), lambda b, g: (b, 0))
    return pl.pallas_call(
        functools.partial(_attn_kernel, tq=tq),
        grid=(batch, groups),
        in_specs=[blk(0), blk(groups), blk(2 * groups), aug, aug],
        out_specs=blk(0),
        out_shape=jax.ShapeDtypeStruct((batch, n_heads, seq, dh), BF16),
        scratch_shapes=[pltpu.VMEM((hp, seq, dh + LANES), BF16)] * 3,
        compiler_params=_params("parallel", "arbitrary"),
        name="fox_attn",
    )(qkv, qkv, qkv, qa, ka)


def _mix_out_kernel(h_ref, y_ref, ga_ref, yc_ref, wout_ref, gmix_ref, gq_ref, wmq_ref,
                    km_ref, vm_ref, wmo_ref, gmem_ref, gnext_ref, o_ref, un_ref, *, sub):
    n_heads, tm, dh = y_ref.shape
    tiles = [slice(t * sub, (t + 1) * sub) for t in range(tm // sub)]
    mm = lambda x, w_ref: jnp.dot(x, w_ref[...], preferred_element_type=F32)

    def merge(rs):
        parts = []
        for hd in range(n_heads):
            sl = slice(hd * dh, (hd + 1) * dh)
            part = (ga_ref[rs, sl].astype(F32) * y_ref[hd, rs, :].astype(F32)
                    + yc_ref[rs, sl].astype(F32))
            parts.append(part.astype(BF16))
        return jnp.concatenate(parts, axis=-1)

    def mem_attention(qm):
        heads = []
        for hd in range(wmq_ref.shape[1] // MEM_HEAD_DIM):
            sl = slice(hd * MEM_HEAD_DIM, (hd + 1) * MEM_HEAD_DIM)
            s = lax.dot_general(qm[:, sl], km_ref[:, sl], NT_DIMS, preferred_element_type=F32)
            p = jnp.exp2(s - jnp.max(s, axis=-1, keepdims=True))
            l = jnp.sum(p, axis=-1, keepdims=True)
            o = jnp.dot(p.astype(BF16), vm_ref[:, sl], preferred_element_type=F32)
            heads.append((o / l).astype(BF16))
        return jnp.concatenate(heads, axis=-1)

    r = [mm(merge(rs), wout_ref) for rs in tiles]
    h2 = [h_ref[rs, :] + _rms(x, gmix_ref[...]) for rs, x in zip(tiles, r)]
    um = [_rms(x, gq_ref[...]).astype(BF16) for x in h2]
    qm = [(mm(x, wmq_ref) * (MEM_HEAD_DIM ** -0.5 * LOG2E)).astype(BF16) for x in um]
    om = [mem_attention(x) for x in qm]
    mo = [mm(x, wmo_ref) for x in om]
    h3 = [a + _rms(b, gmem_ref[...]) for a, b in zip(h2, mo)]
    for rs, x in zip(tiles, h3):
        o_ref[rs, :] = x
        un_ref[rs, :] = _rms(x, gnext_ref[...]).astype(BF16)


def _mix_out(h, y, ga, yc, kv, w_out, g_mix, g_q, w_mq, w_mo, g_mem, g_next, *, seq, tm, sub):
    n, d = h.shape
    n_heads, dh = y.shape[1], y.shape[3]
    n_mem = kv.shape[0] // (n // seq)
    mw = w_mq.shape[1]
    tiles_per_seq = seq // tm
    rows = pl.BlockSpec((tm, d), lambda i: (i, 0))
    return pl.pallas_call(
        functools.partial(_mix_out_kernel, sub=sub),
        grid=(n // tm,),
        in_specs=[
            rows,
            pl.BlockSpec((None, n_heads, tm, dh),
                         lambda i: (i // tiles_per_seq, 0, i % tiles_per_seq, 0)),
            rows, rows,
            _resident((d, d)), _resident((1, d)),
            _resident((1, d)), _resident((d, mw)),
            pl.BlockSpec((n_mem, mw), lambda i: (i // tiles_per_seq, 0)),
            pl.BlockSpec((n_mem, mw), lambda i: (i // tiles_per_seq, 1)),
            _resident((mw, d)), _resident((1, d)), _resident((1, d)),
        ],
        out_specs=[rows, rows],
        out_shape=[jax.ShapeDtypeStruct((n, d), F32), jax.ShapeDtypeStruct((n, d), BF16)],
        compiler_params=_params("parallel"),
        name="mix_out",
    )(h, y, ga, yc, w_out, g_mix, g_q, w_mq, kv, kv, w_mo, g_mem, g_next)


def _mem_kv_kernel(x_ref, g_ref, w_ref, o_ref):
    xn = _rms(x_ref[...], g_ref[...]).astype(BF16)
    o_ref[...] = jnp.dot(xn, w_ref[...], preferred_element_type=F32).astype(BF16)


def _mem_kv(mem, g, w, *, tm):
    n, d = mem.shape
    cols = w.shape[1]
    return pl.pallas_call(
        _mem_kv_kernel,
        grid=(n // tm,),
        in_specs=[pl.BlockSpec((tm, d), lambda i: (i, 0)), _resident((1, d)),
                  _resident((d, cols))],
        out_specs=pl.BlockSpec((tm, cols), lambda i: (i, 0)),
        out_shape=jax.ShapeDtypeStruct((n, cols), BF16),
        compiler_params=_params("parallel"),
        name="mem_kv",
    )(mem, g, w)


def kernel(x, mem, ffn1_pre_g, ffn1_w13, ffn1_w2, ffn1_post_g, mix_pre_g, w_in, forget_bias,
           gate_bias, conv_w, conv_b, w_out, mix_post_g, mem_q_pre_g, mem_kv_g, w_mq, w_mkv,
           w_mo, mem_post_g, ffn2_pre_g, ffn2_w13, ffn2_w2, ffn2_post_g):
    batch, seq, d = x.shape
    n_mem = mem.shape[1]
    depth = ffn1_w13.shape[0]
    n_heads = forget_bias.shape[1]
    fox_w = n_heads * FOX_HEAD_DIM
    assert n_heads * AUG_PER_HEAD <= LANES and fox_w == d and conv_w.shape[2] == d

    row = lambda v: v.reshape(1, -1).astype(F32)
    h = x.reshape(batch * seq, d)
    mem2 = mem.reshape(batch * n_mem, d)
    for l in range(depth):
        h, u = _ffn(h, ffn1_w13[l].astype(BF16), ffn1_w2[l].astype(BF16),
                    0.5 * row(ffn1_post_g[l]), g_pre=row(ffn1_pre_g[l]),
                    g_next=row(mix_pre_g[l]), tm=512, tf=512)

        w = w_in[l]
        to_bf16 = lambda v: lax.reduce_precision(v, exponent_bits=8, mantissa_bits=7).astype(BF16)
        w_qkv = to_bf16(w[:, :3 * fox_w])
        w_f = jnp.pad(to_bf16(w[:, 3 * fox_w:3 * fox_w + n_heads]),
                      ((0, 0), (0, LANES - n_heads)))
        w_cg = to_bf16(w[:, 3 * fox_w + n_heads:])
        qkv, f = _qkv_proj(u, w_qkv, w_f, batch=batch, seq=seq, tm=1024, tn=2048,
                           q_scale=FOX_HEAD_DIM ** -0.5 * LOG2E)
        yc, ga = _conv_gate(u, w_cg, conv_w[l], row(conv_b[l]),
                            gate_bias[l], seq=seq, tm=1024, tn=512, sub=256)
        fb = jnp.pad(forget_bias[l], (0, LANES - n_heads)).reshape(1, LANES)
        qa, ka = _forget_aug(f, fb, batch=batch, seq=seq, n_heads=n_heads)
        y = _fox_attention(qkv, qa, ka, n_heads=n_heads, tq=256, hp=4)

        kv = _mem_kv(mem2, row(mem_kv_g[l]), w_mkv[l].astype(BF16), tm=1024)
        h, u = _mix_out(h, y, ga, yc, kv, w_out[l].astype(BF16), row(mix_post_g[l]),
                        row(mem_q_pre_g[l]), w_mq[l].astype(BF16), w_mo[l].astype(BF16),
                        row(mem_post_g[l]), row(ffn2_pre_g[l]), seq=seq, tm=512, sub=256)

        h = _ffn(h, ffn2_w13[l].astype(BF16), ffn2_w2[l].astype(BF16),
                 0.5 * row(ffn2_post_g[l]), xn=u, tm=512, tf=512)
    return h.reshape(batch, seq, d)
```

```python
import functools
import math

import jax
import jax.numpy as jnp
from jax import lax
from jax.experimental import pallas as pl
from jax.experimental.pallas import tpu as pltpu

EPS = 1e-6
FOX_HEAD_DIM = 128
MEM_HEAD_DIM = 128
CONV_TAPS = 3

LANES = 128
SUBLANES = 8
VMEM_LIMIT_BYTES = 56 * 1024 * 1024
AUG_PER_HEAD = 8
NEG_BIG = -1e30
LOG2E = math.log2(math.e)

BF16 = jnp.bfloat16
F32 = jnp.float32
NT_DIMS = (((1,), (1,)), ((), ()))


def _rms(x, g):
    ms = jnp.mean(x * x, axis=-1, keepdims=True)
    return x * lax.rsqrt(ms + EPS) * g


def _params(*sem):
    return pltpu.CompilerParams(dimension_semantics=sem, vmem_limit_bytes=VMEM_LIMIT_BYTES)


def _resident(shape):
    return pl.BlockSpec(shape, lambda *_: (0,) * len(shape), pipeline_mode=pl.Buffered(1))


def _ffn_kernel(*refs, has_xn, emit_next, n_pairs, single_at):
    refs = list(refs)
    x_ref = refs.pop(0)
    xin_ref = refs.pop(0) if has_xn else None
    gpre_ref = None if has_xn else refs.pop(0)
    w1a_ref, w3a_ref, w2a_ref, w1b_ref, w3b_ref, w2b_ref, gpost_ref = refs[:7]
    refs = refs[7:]
    gnext_ref = refs.pop(0) if emit_next else None
    o_ref = refs.pop(0)
    un_ref = refs.pop(0) if emit_next else None
    xn_ref = xin_ref if has_xn else refs.pop(0)
    j = pl.program_id(1)

    def chunk(xn, w1_ref, w3_ref, w2_ref):
        gate = jnp.dot(xn, w1_ref[...], preferred_element_type=F32)
        up = jnp.dot(xn, w3_ref[...], preferred_element_type=F32)
        act = (gate * jax.nn.sigmoid(gate) * up).astype(BF16)
        return jnp.dot(act, w2_ref[...], preferred_element_type=F32)

    def pair(xn):
        return chunk(xn, w1a_ref, w3a_ref, w2a_ref) + chunk(xn, w1b_ref, w3b_ref, w2b_ref)

    @pl.when(j == 0)
    def _():
        if has_xn:
            xn = xn_ref[...]
        else:
            xn = _rms(x_ref[...], gpre_ref[...]).astype(BF16)
            xn_ref[...] = xn
        o_ref[...] = pair(xn)

    @pl.when((j > 0) & (j < n_pairs) & (j != single_at))
    def _():
        o_ref[...] += pair(xn_ref[...])

    @pl.when(j == single_at)
    def _():
        o_ref[...] += chunk(xn_ref[...], w1a_ref, w3a_ref, w2a_ref)

    @pl.when(j == n_pairs)
    def _():
        f = o_ref[...] + pair(xn_ref[...])
        h = x_ref[...] + _rms(f, gpost_ref[...])
        o_ref[...] = h
        if emit_next:
            un_ref[...] = _rms(h, gnext_ref[...]).astype(BF16)


def _ffn(x, w13, w2, g_post, *, g_pre=None, xn=None, g_next=None, tm, tf):
    n, d = x.shape
    d_ff = w2.shape[0]
    nf = d_ff // tf
    assert nf % 2 == 1 and nf >= 5, "d_ff chunks are walked as pairs plus one single"
    n_pairs = nf // 2
    single_at = n_pairs // 2
    chunk_a = lambda j: jnp.where(j <= single_at, 2 * j, 2 * j - 1)
    chunk_b = lambda j: jnp.where(j < single_at, 2 * j + 1,
                                  jnp.where(j == single_at, 2 * j + 2, 2 * j))
    has_xn, emit_next = xn is not None, g_next is not None
    rows = pl.BlockSpec((tm, d), lambda i, j: (i, 0))
    vec = pl.BlockSpec((1, d), lambda i, j: (0, 0))
    args, in_specs = [x], [rows]
    if has_xn:
        args.append(xn); in_specs.append(rows)
    else:
        args.append(g_pre); in_specs.append(vec)
    for chunk in (chunk_a, chunk_b):
        args += [w13, w13, w2]
        in_specs += [
            pl.BlockSpec((d, tf), lambda i, j, c=chunk: (0, c(j))),
            pl.BlockSpec((d, tf), lambda i, j, c=chunk: (0, c(j) + nf)),
            pl.BlockSpec((tf, d), lambda i, j, c=chunk: (c(j), 0)),
        ]
    args.append(g_post); in_specs.append(vec)
    out_specs, out_shape = rows, jax.ShapeDtypeStruct((n, d), F32)
    if emit_next:
        args.append(g_next); in_specs.append(vec)
        out_specs = [out_specs, rows]
        out_shape = [out_shape, jax.ShapeDtypeStruct((n, d), BF16)]
    body = functools.partial(_ffn_kernel, has_xn=has_xn, emit_next=emit_next, n_pairs=n_pairs,
                             single_at=single_at)
    return pl.pallas_call(
        body,
        grid=(n // tm, n_pairs + 1),
        in_specs=in_specs,
        out_specs=out_specs,
        out_shape=out_shape,
        scratch_shapes=[] if has_xn else [pltpu.VMEM((tm, d), BF16)],
        compiler_params=_params("parallel", "arbitrary"),
        name="ffn",
    )(*args)


def _qkv_kernel(u_ref, wt_ref, wft_ref, o_ref, f_ref, *, n_scaled, scale):
    j = pl.program_id(1)
    proj = lambda wt: lax.dot_general(u_ref[...], wt, NT_DIMS, preferred_element_type=F32)

    @pl.when(j == 0)
    def _():
        f_ref[...] = proj(wft_ref[...])

    acc = proj(wt_ref[...]) * jnp.where(j < n_scaled, scale, 1.0).astype(F32)
    for hh in range(o_ref.shape[0]):
        o_ref[hh] = acc[:, hh * LANES:(hh + 1) * LANES].astype(BF16)


def _qkv_proj(u, wt, wt_f, *, n_cols, batch, seq, tm, tn, q_scale):
    n, d = u.shape
    cols = n_cols
    tiles_per_seq = seq // tm
    heads_per_blk = tn // FOX_HEAD_DIM
    body = functools.partial(_qkv_kernel, n_scaled=cols // 3 // tn, scale=q_scale)
    return pl.pallas_call(
        body,
        grid=(n // tm, cols // tn),
        in_specs=[
            pl.BlockSpec((tm, d), lambda i, j: (i, 0)),
            pl.BlockSpec((tn, d), lambda i, j: (j, 0)),
            pl.BlockSpec((LANES, d), lambda i, j: (0, 0)),
        ],
        out_specs=[
            pl.BlockSpec((None, heads_per_blk, tm, FOX_HEAD_DIM),
                         lambda i, j: (i // tiles_per_seq, j, i % tiles_per_seq, 0)),
            pl.BlockSpec((tm, LANES), lambda i, j: (i, 0)),
        ],
        out_shape=[
            jax.ShapeDtypeStruct((batch, cols // FOX_HEAD_DIM, seq, FOX_HEAD_DIM), BF16),
            jax.ShapeDtypeStruct((n, LANES), F32),
        ],
        compiler_params=_params("parallel", "arbitrary"),
        name="qkv_proj",
    )(u, wt, wt_f)


def _conv_gate_kernel(u_ref, wcx_ref, wcb_ref, wcc_ref, wga_ref, wgb_ref, cw_ref, cbias_ref,
                      gbias_ref, yc_ref, ga_ref, halo_ref, *, tiles_per_seq, sub):
    i, j = pl.program_id(0), pl.program_id(1)
    tm, tn = yc_ref.shape

    @pl.when((i == 0) & (j == 0))
    def _():
        halo_ref[...] = jnp.zeros_like(halo_ref)

    u = u_ref[...]
    top = lax.broadcasted_iota(jnp.int32, (SUBLANES, 1), 0)
    for c in range(tn // sub):
        cs = slice(c * sub, (c + 1) * sub)
        proj = lambda wt_ref: lax.dot_general(u, wt_ref[cs, :], NT_DIMS,
                                              preferred_element_type=F32)
        z = proj(wcc_ref) * proj(wcx_ref)
        zh = jnp.where(i % tiles_per_seq == 0, 0.0, halo_ref[j, :, cs])
        halo_ref[j, :, cs] = z[tm - SUBLANES:, :]

        w0, w1, w2 = cw_ref[0:1, cs], cw_ref[1:2, cs], cw_ref[2:3, cs]
        cbias = cbias_ref[:, cs]
        conv = cbias + w2 * z + w1 * pltpu.roll(z, 1, axis=0) + w0 * pltpu.roll(z, 2, axis=0)
        zt = z[:SUBLANES, :]
        conv_top = (cbias + w2 * zt
                    + w1 * jnp.where(top < 1, pltpu.roll(zh, 1, axis=0), pltpu.roll(zt, 1, axis=0))
                    + w0 * jnp.where(top < 2, pltpu.roll(zh, 2, axis=0), pltpu.roll(zt, 2, axis=0)))

        gated = jax.nn.sigmoid(proj(wgb_ref) + gbias_ref[1:2, cs]) * proj(wcb_ref)
        yc_ref[:, cs] = (gated * conv).astype(BF16)
        yc_ref[0:SUBLANES, cs] = (gated[:SUBLANES, :] * conv_top).astype(BF16)
        ga_ref[:, cs] = jax.nn.sigmoid(proj(wga_ref) + gbias_ref[0:1, cs]).astype(BF16)


def _conv_gate(u, wt, conv_w, conv_b, gate_bias, *, row0, seq, tm, tn, sub):
    n, d = u.shape
    width = conv_w.shape[1]
    nj = width // tn
    wspec = lambda g: pl.BlockSpec((pl.Element(tn), pl.Element(d)),
                                   lambda i, j: (pl.multiple_of(row0 + (g * nj + j) * tn, 16), 0))
    cols = lambda r: pl.BlockSpec((r, tn), lambda i, j: (0, j))
    out = pl.BlockSpec((tm, tn), lambda i, j: (i, j))
    shape = jax.ShapeDtypeStruct((n, width), BF16)
    return pl.pallas_call(
        functools.partial(_conv_gate_kernel, tiles_per_seq=seq // tm, sub=sub),
        grid=(n // tm, nj),
        in_specs=[pl.BlockSpec((tm, d), lambda i, j: (i, 0)),
                  wspec(0), wspec(1), wspec(2), wspec(3), wspec(4),
                  cols(CONV_TAPS), cols(1), cols(2)],
        out_specs=[out, out],
        out_shape=[shape, shape],
        scratch_shapes=[pltpu.VMEM((nj, SUBLANES, tn), F32)],
        compiler_params=_params("arbitrary", "arbitrary"),
        name="conv_gate",
    )(u, wt, wt, wt, wt, wt, conv_w, conv_b, gate_bias)


def _split3(x):
    hi = x.astype(BF16)
    r = x - hi.astype(F32)
    mid = r.astype(BF16)
    lo = (r - mid.astype(F32)).astype(BF16)
    return hi, mid, lo


def _forget_kernel(f_ref, b_ref, qa_ref, ka_ref, *, blk, n_heads):
    s = f_ref.shape[0]
    x = f_ref[...] + b_ref[...]
    logf = jnp.minimum(x, 0.0) - jnp.log1p(jnp.exp(-jnp.abs(x)))

    r = lax.broadcasted_iota(jnp.int32, (blk, blk), 0)
    c = lax.broadcasted_iota(jnp.int32, (blk, blk), 1)
    tri = (r >= c).astype(BF16)

    er = lax.broadcasted_iota(jnp.int32, (LANES, LANES), 0)
    ec = lax.broadcasted_iota(jnp.int32, (LANES, LANES), 1)
    lane = lax.broadcasted_iota(jnp.int32, (1, LANES), 1)
    slot = lane % AUG_PER_HEAD
    live = lane < n_heads * AUG_PER_HEAD

    def place(k):
        return ((ec == AUG_PER_HEAD * er + k) & (er < n_heads)).astype(BF16)

    spread = [jnp.concatenate([place(k), -place(k + 3)], axis=1) for k in range(3)]
    ones = jnp.concatenate([((slot >= 3) & (slot < 6) & live).astype(F32),
                            ((slot < 3) & live).astype(F32)], axis=1)

    carry = jnp.zeros((1, LANES), F32)
    for b in range(0, s // blk, 2):
        rows = [slice((b + t) * blk, (b + t + 1) * blk) for t in range(2)]
        parts = [_split3(logf[rs]) for rs in rows]
        local = sum(jnp.dot(tri, jnp.concatenate([parts[0][p], parts[1][p]], axis=1),
                            preferred_element_type=F32) for p in range(3))
        for t, rs in enumerate(rows):
            cum = local[:, t * LANES:(t + 1) * LANES] + carry
            carry = cum[blk - 1:blk, :]
            c3 = _split3(cum * LOG2E)
            aug = ones + sum(jnp.dot(c3[k], spread[k], preferred_element_type=F32)
                             for k in range(3))
            qa_ref[rs, :] = aug[:, :LANES].astype(BF16)
            ka_ref[rs, :] = aug[:, LANES:].astype(BF16)


def _forget_aug(f, bias_row, *, batch, seq, n_heads):
    body = functools.partial(_forget_kernel, blk=256, n_heads=n_heads)
    spec = pl.BlockSpec((seq, LANES), lambda b: (b, 0))
    shape = jax.ShapeDtypeStruct((batch * seq, LANES), BF16)
    return pl.pallas_call(
        body,
        grid=(batch,),
        in_specs=[spec, pl.BlockSpec((1, LANES), lambda b: (0, 0))],
        out_specs=[spec, spec],
        out_shape=[shape, shape],
        compiler_params=_params("parallel"),
        name="forget_aug",
    )(f, bias_row)


def _attn_kernel(q_ref, k_ref, v_ref, qa_ref, ka_ref, o_ref, qc_ref, kc_ref, vc_ref, *, tq):
    hp, s, dh = q_ref.shape
    heads = range(hp)
    lane = lax.broadcasted_iota(jnp.int32, (1, LANES), 1)
    ones_col = jnp.broadcast_to((lane == 0).astype(BF16), (s, LANES))
    for a in heads:
        mine = (lane // AUG_PER_HEAD) == pl.program_id(1) * hp + a
        qc_ref[a, :, :dh] = q_ref[a]
        qc_ref[a, :, dh:] = qa_ref[...]
        kc_ref[a, :, :dh] = k_ref[a]
        kc_ref[a, :, dh:] = jnp.where(mine, ka_ref[...], jnp.zeros_like(ka_ref[...]))
        vc_ref[a, :, :dh] = v_ref[a]
        vc_ref[a, :, dh:] = ones_col

    row = lax.broadcasted_iota(jnp.int32, (tq, tq), 0)
    col = lax.broadcasted_iota(jnp.int32, (tq, tq), 1)
    causal = col <= row
    qk = lambda x, y: lax.dot_general(x, y, NT_DIMS, preferred_element_type=F32)
    pv = lambda x, y: jnp.dot(x.astype(BF16), y, preferred_element_type=F32)
    rowmax = lambda x: jnp.max(x, axis=-1, keepdims=True)

    for i in reversed(range(s // tq)):
        lo = i * tq
        qi = [qc_ref[a, lo:lo + tq, :] for a in heads]
        sd = [jnp.where(causal, qk(qi[a], kc_ref[a, lo:lo + tq, :]), NEG_BIG) for a in heads]
        m = [rowmax(x) for x in sd]
        if i > 0:
            so = [qk(qi[a], kc_ref[a, 0:lo, :]) for a in heads]
            m = [jnp.maximum(m[a], rowmax(so[a])) for a in heads]
        acc = [pv(jnp.exp2(sd[a] - m[a]), vc_ref[a, lo:lo + tq, :]) for a in heads]
        if i > 0:
            acc = [acc[a] + pv(jnp.exp2(so[a] - m[a]), vc_ref[a, 0:lo, :]) for a in heads]
        for a in heads:
            o_ref[a, lo:lo + tq, :] = (acc[a][:, :dh] / acc[a][:, dh:dh + 1]).astype(o_ref.dtype)


def _fox_attention(qkv, qa, ka, *, n_heads, tq, hp):
    batch, _, seq, dh = qkv.shape
    groups = n_heads // hp
    blk = lambda off: pl.BlockSpec((None, hp, seq, dh), lambda b, g: (b, off + g, 0, 0))
    aug = pl.BlockSpec((seq, LANES), lambda b, g: (b, 0))
    return pl.pallas_call(
        functools.partial(_attn_kernel, tq=tq),
        grid=(batch, groups),
        in_specs=[blk(0), blk(groups), blk(2 * groups), aug, aug],
        out_specs=blk(0),
        out_shape=jax.ShapeDtypeStruct((batch, n_heads, seq, dh), BF16),
        scratch_shapes=[pltpu.VMEM((hp, seq, dh + LANES), BF16)] * 3,
        compiler_params=_params("parallel", "arbitrary"),
        name="fox_attn",
    )(qkv, qkv, qkv, qa, ka)


def _mix_out_kernel(h_ref, y_ref, ga_ref, yc_ref, wout_ref, gmix_ref, gq_ref, wmq_ref,
                    km_ref, vm_ref, wmo_ref, gmem_ref, gnext_ref, o_ref, un_ref, *, sub):
    n_heads, tm, dh = y_ref.shape
    tiles = [slice(t * sub, (t + 1) * sub) for t in range(tm // sub)]
    mm = lambda x, w_ref: jnp.dot(x, w_ref[...], preferred_element_type=F32)

    def merge(rs):
        parts = []
        for hd in range(n_heads):
            sl = slice(hd * dh, (hd + 1) * dh)
            part = (ga_ref[rs, sl].astype(F32) * y_ref[hd, rs, :].astype(F32)
                    + yc_ref[rs, sl].astype(F32))
            parts.append(part.astype(BF16))
        return jnp.concatenate(parts, axis=-1)

    def mem_attention(qm):
        heads = []
        for hd in range(wmq_ref.shape[1] // MEM_HEAD_DIM):
            sl = slice(hd * MEM_HEAD_DIM, (hd + 1) * MEM_HEAD_DIM)
            s = lax.dot_general(qm[:, sl], km_ref[:, sl], NT_DIMS, preferred_element_type=F32)
            p = jnp.exp2(s - jnp.max(s, axis=-1, keepdims=True))
            l = jnp.sum(p, axis=-1, keepdims=True)
            o = jnp.dot(p.astype(BF16), vm_ref[:, sl], preferred_element_type=F32)
            heads.append((o / l).astype(BF16))
        return jnp.concatenate(heads, axis=-1)

    r = [mm(merge(rs), wout_ref) for rs in tiles]
    h2 = [h_ref[rs, :] + _rms(x, gmix_ref[...]) for rs, x in zip(tiles, r)]
    um = [_rms(x, gq_ref[...]).astype(BF16) for x in h2]
    qm = [(mm(x, wmq_ref) * (MEM_HEAD_DIM ** -0.5 * LOG2E)).astype(BF16) for x in um]
    om = [mem_attention(x) for x in qm]
    mo = [mm(x, wmo_ref) for x in om]
    h3 = [a + _rms(b, gmem_ref[...]) for a, b in zip(h2, mo)]
    for rs, x in zip(tiles, h3):
        o_ref[rs, :] = x
        un_ref[rs, :] = _rms(x, gnext_ref[...]).astype(BF16)


def _mix_out(h, y, ga, yc, kv, w_out, g_mix, g_q, w_mq, w_mo, g_mem, g_next, *, seq, tm, sub):
    n, d = h.shape
    n_heads, dh = y.shape[1], y.shape[3]
    n_mem = kv.shape[0] // (n // seq)
    mw = w_mq.shape[1]
    tiles_per_seq = seq // tm
    rows = pl.BlockSpec((tm, d), lambda i: (i, 0))
    return pl.pallas_call(
        functools.partial(_mix_out_kernel, sub=sub),
        grid=(n // tm,),
        in_specs=[
            rows,
            pl.BlockSpec((None, n_heads, tm, dh),
                         lambda i: (i // tiles_per_seq, 0, i % tiles_per_seq, 0)),
            rows, rows,
            _resident((d, d)), _resident((1, d)),
            _resident((1, d)), _resident((d, mw)),
            pl.BlockSpec((n_mem, mw), lambda i: (i // tiles_per_seq, 0)),
            pl.BlockSpec((n_mem, mw), lambda i: (i // tiles_per_seq, 1)),
            _resident((mw, d)), _resident((1, d)), _resident((1, d)),
        ],
        out_specs=[rows, rows],
        out_shape=[jax.ShapeDtypeStruct((n, d), F32), jax.ShapeDtypeStruct((n, d), BF16)],
        compiler_params=_params("parallel"),
        name="mix_out",
    )(h, y, ga, yc, w_out, g_mix, g_q, w_mq, kv, kv, w_mo, g_mem, g_next)


def _mem_kv_kernel(x_ref, g_ref, w_ref, o_ref):
    xn = _rms(x_ref[...], g_ref[...]).astype(BF16)
    o_ref[...] = jnp.dot(xn, w_ref[...], preferred_element_type=F32).astype(BF16)


def _mem_kv(mem, g, w, *, tm):
    n, d = mem.shape
    cols = w.shape[1]
    return pl.pallas_call(
        _mem_kv_kernel,
        grid=(n // tm,),
        in_specs=[pl.BlockSpec((tm, d), lambda i: (i, 0)), _resident((1, d)),
                  _resident((d, cols))],
        out_specs=pl.BlockSpec((tm, cols), lambda i: (i, 0)),
        out_shape=jax.ShapeDtypeStruct((n, cols), BF16),
        compiler_params=_params("parallel"),
        name="mem_kv",
    )(mem, g, w)


def kernel(x, mem, ffn1_pre_g, ffn1_w13, ffn1_w2, ffn1_post_g, mix_pre_g, w_in, forget_bias,
           gate_bias, conv_w, conv_b, w_out, mix_post_g, mem_q_pre_g, mem_kv_g, w_mq, w_mkv,
           w_mo, mem_post_g, ffn2_pre_g, ffn2_w13, ffn2_w2, ffn2_post_g):
    batch, seq, d = x.shape
    n_mem = mem.shape[1]
    depth = ffn1_w13.shape[0]
    n_heads = forget_bias.shape[1]
    fox_w = n_heads * FOX_HEAD_DIM
    assert n_heads * AUG_PER_HEAD <= LANES and fox_w == d and conv_w.shape[2] == d

    row = lambda v: v.reshape(1, -1).astype(F32)
    h = x.reshape(batch * seq, d)
    mem2 = mem.reshape(batch * n_mem, d)
    for l in range(depth):
        h, u = _ffn(h, ffn1_w13[l].astype(BF16), ffn1_w2[l].astype(BF16),
                    0.5 * row(ffn1_post_g[l]), g_pre=row(ffn1_pre_g[l]),
                    g_next=row(mix_pre_g[l]), tm=512, tf=512)

        wt = w_in[l].T.astype(BF16)
        wt_f = jnp.pad(wt[3 * fox_w:3 * fox_w + n_heads], ((0, LANES - n_heads), (0, 0)))
        qkv, f = _qkv_proj(u, wt, wt_f, n_cols=3 * fox_w, batch=batch, seq=seq, tm=1024,
                           tn=2048, q_scale=FOX_HEAD_DIM ** -0.5 * LOG2E)
        yc, ga = _conv_gate(u, wt, conv_w[l], row(conv_b[l]), gate_bias[l],
                            row0=3 * fox_w + n_heads, seq=seq, tm=1024, tn=512, sub=256)
        fb = jnp.pad(forget_bias[l], (0, LANES - n_heads)).reshape(1, LANES)
        qa, ka = _forget_aug(f, fb, batch=batch, seq=seq, n_heads=n_heads)
        y = _fox_attention(qkv, qa, ka, n_heads=n_heads, tq=256, hp=4)

        kv = _mem_kv(mem2, row(mem_kv_g[l]), w_mkv[l].astype(BF16), tm=1024)
        h, u = _mix_out(h, y, ga, yc, kv, w_out[l].astype(BF16), row(mix_post_g[l]),
                        row(mem_q_pre_g[l]), w_mq[l].astype(BF16), w_mo[l].astype(BF16),
                        row(mem_post_g[l]), row(ffn2_pre_g[l]), seq=seq, tm=512, sub=256)

        h = _ffn(h, ffn2_w13[l].astype(BF16), ffn2_w2[l].astype(BF16),
                 0.5 * row(ffn2_post_g[l]), xn=u, tm=512, tf=512)
    return h.reshape(batch, seq, d)
```

```python
import functools
import math

import jax
import jax.numpy as jnp
from jax import lax
from jax.experimental import pallas as pl
from jax.experimental.pallas import tpu as pltpu

EPS = 1e-6
FOX_HEAD_DIM = 128
MEM_HEAD_DIM = 128
CONV_TAPS = 3

LANES = 128
SUBLANES = 8
VMEM_LIMIT_BYTES = 56 * 1024 * 1024
AUG_PER_HEAD = 8
NEG_BIG = -1e30
LOG2E = math.log2(math.e)

BF16 = jnp.bfloat16
F32 = jnp.float32
NT_DIMS = (((1,), (1,)), ((), ()))


def _rms(x, g):
    ms = jnp.mean(x * x, axis=-1, keepdims=True)
    return x * lax.rsqrt(ms + EPS) * g


def _params(*sem):
    return pltpu.CompilerParams(dimension_semantics=sem, vmem_limit_bytes=VMEM_LIMIT_BYTES)


def _resident(shape):
    return pl.BlockSpec(shape, lambda *_: (0,) * len(shape), pipeline_mode=pl.Buffered(1))


def _ffn_kernel(x_ref, gpre_ref, w1a_ref, w3a_ref, w2a_ref, w1b_ref, w3b_ref, w2b_ref, gpost_ref,
                *rest, emit_next, n_pairs, single_at):
    if emit_next:
        gnext_ref, o_ref, un_ref, xn_ref = rest
    else:
        o_ref, xn_ref = rest
    j = pl.program_id(1)

    def chunk(xn, w1_ref, w3_ref, w2_ref):
        gate = jnp.dot(xn, w1_ref[...], preferred_element_type=F32)
        up = jnp.dot(xn, w3_ref[...], preferred_element_type=F32)
        act = (gate * jax.nn.sigmoid(gate) * up).astype(BF16)
        return jnp.dot(act, w2_ref[...], preferred_element_type=F32)

    def pair(xn):
        return chunk(xn, w1a_ref, w3a_ref, w2a_ref) + chunk(xn, w1b_ref, w3b_ref, w2b_ref)

    @pl.when(j == 0)
    def _():
        xn = _rms(x_ref[...], gpre_ref[...]).astype(BF16)
        xn_ref[...] = xn
        o_ref[...] = pair(xn)

    @pl.when((j > 0) & (j < n_pairs) & (j != single_at))
    def _():
        o_ref[...] += pair(xn_ref[...])

    @pl.when(j == single_at)
    def _():
        o_ref[...] += chunk(xn_ref[...], w1a_ref, w3a_ref, w2a_ref)

    @pl.when(j == n_pairs)
    def _():
        half = o_ref.shape[0] // 2
        for rs in (slice(0, half), slice(half, 2 * half)):
            f = o_ref[rs, :] + pair(xn_ref[rs, :])
            h = x_ref[rs, :] + _rms(f, gpost_ref[...])
            o_ref[rs, :] = h
            if emit_next:
                un_ref[rs, :] = _rms(h, gnext_ref[...]).astype(BF16)


def _ffn(x, g_pre, w13, w2, g_post, *, g_next=None, tm, tf):
    n, d = x.shape
    d_ff = w2.shape[0]
    nf = d_ff // tf
    assert nf % 2 == 1 and nf >= 5, "d_ff chunks are walked as pairs plus one single"
    n_pairs = nf // 2
    single_at = n_pairs // 2
    chunk_a = lambda j: jnp.where(j <= single_at, 2 * j, 2 * j - 1)
    chunk_b = lambda j: jnp.where(j < single_at, 2 * j + 1,
                                  jnp.where(j == single_at, 2 * j + 2, 2 * j))
    emit_next = g_next is not None
    rows = pl.BlockSpec((tm, d), lambda i, j: (i, 0))
    vec = pl.BlockSpec((1, d), lambda i, j: (0, 0))
    args, in_specs = [x, g_pre], [rows, vec]
    for chunk in (chunk_a, chunk_b):
        args += [w13, w13, w2]
        in_specs += [
            pl.BlockSpec((d, tf), lambda i, j, c=chunk: (0, c(j))),
            pl.BlockSpec((d, tf), lambda i, j, c=chunk: (0, c(j) + nf)),
            pl.BlockSpec((tf, d), lambda i, j, c=chunk: (c(j), 0)),
        ]
    args.append(g_post); in_specs.append(vec)
    out_specs, out_shape = rows, jax.ShapeDtypeStruct((n, d), F32)
    if emit_next:
        args.append(g_next); in_specs.append(vec)
        out_specs = [out_specs, rows]
        out_shape = [out_shape, jax.ShapeDtypeStruct((n, d), BF16)]
    body = functools.partial(_ffn_kernel, emit_next=emit_next, n_pairs=n_pairs,
                             single_at=single_at)
    return pl.pallas_call(
        body,
        grid=(n // tm, n_pairs + 1),
        in_specs=in_specs,
        out_specs=out_specs,
        out_shape=out_shape,
        scratch_shapes=[pltpu.VMEM((tm, d), BF16)],
        compiler_params=_params("parallel", "arbitrary"),
        name="ffn",
    )(*args)


def _qkv_kernel(u_ref, wt_ref, wft_ref, o_ref, f_ref, *, n_scaled, scale):
    j = pl.program_id(1)
    proj = lambda wt: lax.dot_general(u_ref[...], wt, NT_DIMS, preferred_element_type=F32)

    @pl.when(j == 0)
    def _():
        f_ref[...] = proj(wft_ref[...])

    acc = proj(wt_ref[...]) * jnp.where(j < n_scaled, scale, 1.0).astype(F32)
    for hh in range(o_ref.shape[0]):
        o_ref[hh] = acc[:, hh * LANES:(hh + 1) * LANES].astype(BF16)


def _qkv_proj(u, wt, wt_f, *, n_cols, batch, seq, tm, tn, q_scale):
    n, d = u.shape
    cols = n_cols
    tiles_per_seq = seq // tm
    heads_per_blk = tn // FOX_HEAD_DIM
    body = functools.partial(_qkv_kernel, n_scaled=cols // 3 // tn, scale=q_scale)
    return pl.pallas_call(
        body,
        grid=(n // tm, cols // tn),
        in_specs=[
            pl.BlockSpec((tm, d), lambda i, j: (i, 0)),
            pl.BlockSpec((tn, d), lambda i, j: (j, 0)),
            pl.BlockSpec((LANES, d), lambda i, j: (0, 0)),
        ],
        out_specs=[
            pl.BlockSpec((None, heads_per_blk, tm, FOX_HEAD_DIM),
                         lambda i, j: (i // tiles_per_seq, j, i % tiles_per_seq, 0)),
            pl.BlockSpec((tm, LANES), lambda i, j: (i, 0)),
        ],
        out_shape=[
            jax.ShapeDtypeStruct((batch, cols // FOX_HEAD_DIM, seq, FOX_HEAD_DIM), BF16),
            jax.ShapeDtypeStruct((n, LANES), F32),
        ],
        compiler_params=_params("parallel", "arbitrary"),
        name="qkv_proj",
    )(u, wt, wt_f)


def _conv_gate_kernel(u_ref, wcx_ref, wcb_ref, wcc_ref, wga_ref, wgb_ref, cw_ref, cbias_ref,
                      gbias_ref, yc_ref, ga_ref, halo_ref, *, tiles_per_seq, sub):
    i, j = pl.program_id(0), pl.program_id(1)
    tm, tn = yc_ref.shape

    @pl.when((i == 0) & (j == 0))
    def _():
        halo_ref[...] = jnp.zeros_like(halo_ref)

    u = u_ref[...]
    top = lax.broadcasted_iota(jnp.int32, (SUBLANES, 1), 0)
    for c in range(tn // sub):
        cs = slice(c * sub, (c + 1) * sub)
        proj = lambda wt_ref: lax.dot_general(u, wt_ref[cs, :], NT_DIMS,
                                              preferred_element_type=F32)
        z = proj(wcc_ref) * proj(wcx_ref)
        zh = jnp.where(i % tiles_per_seq == 0, 0.0, halo_ref[j, :, cs])
        halo_ref[j, :, cs] = z[tm - SUBLANES:, :]

        w0, w1, w2 = cw_ref[0:1, cs], cw_ref[1:2, cs], cw_ref[2:3, cs]
        cbias = cbias_ref[:, cs]
        conv = cbias + w2 * z + w1 * pltpu.roll(z, 1, axis=0) + w0 * pltpu.roll(z, 2, axis=0)
        zt = z[:SUBLANES, :]
        conv_top = (cbias + w2 * zt
                    + w1 * jnp.where(top < 1, pltpu.roll(zh, 1, axis=0), pltpu.roll(zt, 1, axis=0))
                    + w0 * jnp.where(top < 2, pltpu.roll(zh, 2, axis=0), pltpu.roll(zt, 2, axis=0)))

        gated = jax.nn.sigmoid(proj(wgb_ref) + gbias_ref[1:2, cs]) * proj(wcb_ref)
        yc_ref[:, cs] = (gated * conv).astype(BF16)
        yc_ref[0:SUBLANES, cs] = (gated[:SUBLANES, :] * conv_top).astype(BF16)
        ga_ref[:, cs] = jax.nn.sigmoid(proj(wga_ref) + gbias_ref[0:1, cs]).astype(BF16)


def _conv_gate(u, wt, conv_w, conv_b, gate_bias, *, row0, seq, tm, tn, sub):
    n, d = u.shape
    width = conv_w.shape[1]
    nj = width // tn
    wspec = lambda g: pl.BlockSpec((pl.Element(tn), pl.Element(d)),
                                   lambda i, j: (pl.multiple_of(row0 + (g * nj + j) * tn, 16), 0))
    cols = lambda r: pl.BlockSpec((r, tn), lambda i, j: (0, j))
    out = pl.BlockSpec((tm, tn), lambda i, j: (i, j))
    shape = jax.ShapeDtypeStruct((n, width), BF16)
    return pl.pallas_call(
        functools.partial(_conv_gate_kernel, tiles_per_seq=seq // tm, sub=sub),
        grid=(n // tm, nj),
        in_specs=[pl.BlockSpec((tm, d), lambda i, j: (i, 0)),
                  wspec(0), wspec(1), wspec(2), wspec(3), wspec(4),
                  cols(CONV_TAPS), cols(1), cols(2)],
        out_specs=[out, out],
        out_shape=[shape, shape],
        scratch_shapes=[pltpu.VMEM((nj, SUBLANES, tn), F32)],
        compiler_params=_params("arbitrary", "arbitrary"),
        name="conv_gate",
    )(u, wt, wt, wt, wt, wt, conv_w, conv_b, gate_bias)


def _split3(x):
    hi = x.astype(BF16)
    r = x - hi.astype(F32)
    mid = r.astype(BF16)
    lo = (r - mid.astype(F32)).astype(BF16)
    return hi, mid, lo


def _forget_kernel(f_ref, b_ref, qa_ref, ka_ref, *, blk, n_heads):
    s = f_ref.shape[0]
    x = f_ref[...] + b_ref[...]
    logf = jnp.minimum(x, 0.0) - jnp.log1p(jnp.exp(-jnp.abs(x)))

    r = lax.broadcasted_iota(jnp.int32, (blk, blk), 0)
    c = lax.broadcasted_iota(jnp.int32, (blk, blk), 1)
    tri = (r >= c).astype(BF16)

    er = lax.broadcasted_iota(jnp.int32, (LANES, LANES), 0)
    ec = lax.broadcasted_iota(jnp.int32, (LANES, LANES), 1)
    lane = lax.broadcasted_iota(jnp.int32, (1, LANES), 1)
    slot = lane % AUG_PER_HEAD
    live = lane < n_heads * AUG_PER_HEAD

    def place(k):
        return ((ec == AUG_PER_HEAD * er + k) & (er < n_heads)).astype(BF16)

    spread = [jnp.concatenate([place(k), -place(k + 3)], axis=1) for k in range(3)]
    ones = jnp.concatenate([((slot >= 3) & (slot < 6) & live).astype(F32),
                            ((slot < 3) & live).astype(F32)], axis=1)

    carry = jnp.zeros((1, LANES), F32)
    for b in range(0, s // blk, 2):
        rows = [slice((b + t) * blk, (b + t + 1) * blk) for t in range(2)]
        parts = [_split3(logf[rs]) for rs in rows]
        local = sum(jnp.dot(tri, jnp.concatenate([parts[0][p], parts[1][p]], axis=1),
                            preferred_element_type=F32) for p in range(3))
        for t, rs in enumerate(rows):
            cum = local[:, t * LANES:(t + 1) * LANES] + carry
            carry = cum[blk - 1:blk, :]
            c3 = _split3(cum * LOG2E)
            aug = ones + sum(jnp.dot(c3[k], spread[k], preferred_element_type=F32)
                             for k in range(3))
            qa_ref[rs, :] = aug[:, :LANES].astype(BF16)
            ka_ref[rs, :] = aug[:, LANES:].astype(BF16)


def _forget_aug(f, bias_row, *, batch, seq, n_heads):
    body = functools.partial(_forget_kernel, blk=256, n_heads=n_heads)
    spec = pl.BlockSpec((seq, LANES), lambda b: (b, 0))
    shape = jax.ShapeDtypeStruct((batch * seq, LANES), BF16)
    return pl.pallas_call(
        body,
        grid=(batch,),
        in_specs=[spec, pl.BlockSpec((1, LANES), lambda b: (0, 0))],
        out_specs=[spec, spec],
        out_shape=[shape, shape],
        compiler_params=_params("parallel"),
        name="forget_aug",
    )(f, bias_row)


def _attn_kernel(q_ref, k_ref, v_ref, qa_ref, ka_ref, o_ref, qc_ref, kc_ref, vc_ref, *, tq):
    hp, s, dh = q_ref.shape
    heads = range(hp)
    lane = lax.broadcasted_iota(jnp.int32, (1, LANES), 1)
    ones_col = jnp.broadcast_to((lane == 0).astype(BF16), (s, LANES))
    for a in heads:
        mine = (lane // AUG_PER_HEAD) == pl.program_id(1) * hp + a
        qc_ref[a, :, :dh] = q_ref[a]
        qc_ref[a, :, dh:] = qa_ref[...]
        kc_ref[a, :, :dh] = k_ref[a]
        kc_ref[a, :, dh:] = jnp.where(mine, ka_ref[...], jnp.zeros_like(ka_ref[...]))
        vc_ref[a, :, :dh] = v_ref[a]
        vc_ref[a, :, dh:] = ones_col

    row = lax.broadcasted_iota(jnp.int32, (tq, tq), 0)
    col = lax.broadcasted_iota(jnp.int32, (tq, tq), 1)
    causal = col <= row
    qk = lambda x, y: lax.dot_general(x, y, NT_DIMS, preferred_element_type=F32)
    pv = lambda x, y: jnp.dot(x.astype(BF16), y, preferred_element_type=F32)
    rowmax = lambda x: jnp.max(x, axis=-1, keepdims=True)

    for i in reversed(range(s // tq)):
        lo = i * tq
        qi = [qc_ref[a, lo:lo + tq, :] for a in heads]
        sd = [jnp.where(causal, qk(qi[a], kc_ref[a, lo:lo + tq, :]), NEG_BIG) for a in heads]
        m = [rowmax(x) for x in sd]
        if i > 0:
            so = [qk(qi[a], kc_ref[a, 0:lo, :]) for a in heads]
            m = [jnp.maximum(m[a], rowmax(so[a])) for a in heads]
        acc = [pv(jnp.exp2(sd[a] - m[a]), vc_ref[a, lo:lo + tq, :]) for a in heads]
        if i > 0:
            acc = [acc[a] + pv(jnp.exp2(so[a] - m[a]), vc_ref[a, 0:lo, :]) for a in heads]
        for a in heads:
            o_ref[a, lo:lo + tq, :] = (acc[a][:, :dh] / acc[a][:, dh:dh + 1]).astype(o_ref.dtype)


def _fox_attention(qkv, qa, ka, *, n_heads, tq, hp):
    batch, _, seq, dh = qkv.shape
    groups = n_heads // hp
    blk = lambda off: pl.BlockSpec((None, hp, seq, dh), lambda b, g: (b, off + g, 0, 0))
    aug = pl.BlockSpec((seq, LANES), lambda b, g: (b, 0))
    return pl.pallas_call(
        functools.partial(_attn_kernel, tq=tq),
        grid=(batch, groups),
        in_specs=[blk(0), blk(groups), blk(2 * groups), aug, aug],
        out_specs=blk(0),
        out_shape=jax.ShapeDtypeStruct((batch, n_heads, seq, dh), BF16),
        scratch_shapes=[pltpu.VMEM((hp, seq, dh + LANES), BF16)] * 3,
        compiler_params=_params("parallel", "arbitrary"),
        name="fox_attn",
    )(qkv, qkv, qkv, qa, ka)


def _mix_out_kernel(h_ref, y_ref, ga_ref, yc_ref, wout_ref, gmix_ref, gq_ref, wmq_ref,
                    km_ref, vm_ref, wmo_ref, gmem_ref, o_ref, *, sub):
    n_heads, tm, dh = y_ref.shape
    tiles = [slice(t * sub, (t + 1) * sub) for t in range(tm // sub)]
    mm = lambda x, w_ref: jnp.dot(x, w_ref[...], preferred_element_type=F32)

    def merge(rs):
        parts = []
        for hd in range(n_heads):
            sl = slice(hd * dh, (hd + 1) * dh)
            part = (ga_ref[rs, sl].astype(F32) * y_ref[hd, rs, :].astype(F32)
                    + yc_ref[rs, sl].astype(F32))
            parts.append(part.astype(BF16))
        return jnp.concatenate(parts, axis=-1)

    def mem_attention(qm):
        heads = []
        for hd in range(wmq_ref.shape[1] // MEM_HEAD_DIM):
            sl = slice(hd * MEM_HEAD_DIM, (hd + 1) * MEM_HEAD_DIM)
            s = lax.dot_general(qm[:, sl], km_ref[:, sl], NT_DIMS, preferred_element_type=F32)
            p = jnp.exp2(s - jnp.max(s, axis=-1, keepdims=True))
            l = jnp.sum(p, axis=-1, keepdims=True)
            o = jnp.dot(p.astype(BF16), vm_ref[:, sl], preferred_element_type=F32)
            heads.append((o / l).astype(BF16))
        return jnp.concatenate(heads, axis=-1)

    r = [mm(merge(rs), wout_ref) for rs in tiles]
    h2 = [h_ref[rs, :] + _rms(x, gmix_ref[...]) for rs, x in zip(tiles, r)]
    um = [_rms(x, gq_ref[...]).astype(BF16) for x in h2]
    qm = [(mm(x, wmq_ref) * (MEM_HEAD_DIM ** -0.5 * LOG2E)).astype(BF16) for x in um]
    om = [mem_attention(x) for x in qm]
    mo = [mm(x, wmo_ref) for x in om]
    for rs, a, b in zip(tiles, h2, mo):
        o_ref[rs, :] = a + _rms(b, gmem_ref[...])


def _mix_out(h, y, ga, yc, kv, w_out, g_mix, g_q, w_mq, w_mo, g_mem, *, seq, tm, sub):
    n, d = h.shape
    n_heads, dh = y.shape[1], y.shape[3]
    n_mem = kv.shape[0] // (n // seq)
    mw = w_mq.shape[1]
    tiles_per_seq = seq // tm
    rows = pl.BlockSpec((tm, d), lambda i: (i, 0))
    return pl.pallas_call(
        functools.partial(_mix_out_kernel, sub=sub),
        grid=(n // tm,),
        in_specs=[
            rows,
            pl.BlockSpec((None, n_heads, tm, dh),
                         lambda i: (i // tiles_per_seq, 0, i % tiles_per_seq, 0)),
            rows, rows,
            _resident((d, d)), _resident((1, d)),
            _resident((1, d)), _resident((d, mw)),
            pl.BlockSpec((n_mem, mw), lambda i: (i // tiles_per_seq, 0)),
            pl.BlockSpec((n_mem, mw), lambda i: (i // tiles_per_seq, 1)),
            _resident((mw, d)), _resident((1, d)),
        ],
        out_specs=rows,
        out_shape=jax.ShapeDtypeStruct((n, d), F32),
        compiler_params=_params("parallel"),
        name="mix_out",
    )(h, y, ga, yc, w_out, g_mix, g_q, w_mq, kv, kv, w_mo, g_mem)


def _mem_kv_kernel(x_ref, g_ref, w_ref, o_ref):
    xn = _rms(x_ref[...], g_ref[...]).astype(BF16)
    o_ref[...] = jnp.dot(xn, w_ref[...], preferred_element_type=F32).astype(BF16)


def _mem_kv(mem, g, w, *, tm):
    n, d = mem.shape
    cols = w.shape[1]
    return pl.pallas_call(
        _mem_kv_kernel,
        grid=(n // tm,),
        in_specs=[pl.BlockSpec((tm, d), lambda i: (i, 0)), _resident((1, d)),
                  _resident((d, cols))],
        out_specs=pl.BlockSpec((tm, cols), lambda i: (i, 0)),
        out_shape=jax.ShapeDtypeStruct((n, cols), BF16),
        compiler_params=_params("parallel"),
        name="mem_kv",
    )(mem, g, w)


def kernel(x, mem, ffn1_pre_g, ffn1_w13, ffn1_w2, ffn1_post_g, mix_pre_g, w_in, forget_bias,
           gate_bias, conv_w, conv_b, w_out, mix_post_g, mem_q_pre_g, mem_kv_g, w_mq, w_mkv,
           w_mo, mem_post_g, ffn2_pre_g, ffn2_w13, ffn2_w2, ffn2_post_g):
    batch, seq, d = x.shape
    n_mem = mem.shape[1]
    depth = ffn1_w13.shape[0]
    n_heads = forget_bias.shape[1]
    fox_w = n_heads * FOX_HEAD_DIM
    assert n_heads * AUG_PER_HEAD <= LANES and fox_w == d and conv_w.shape[2] == d

    row = lambda v: v.reshape(1, -1).astype(F32)
    h = x.reshape(batch * seq, d)
    mem2 = mem.reshape(batch * n_mem, d)
    for l in range(depth):
        h, u = _ffn(h, row(ffn1_pre_g[l]), ffn1_w13[l].astype(BF16), ffn1_w2[l].astype(BF16),
                    0.5 * row(ffn1_post_g[l]), g_next=row(mix_pre_g[l]), tm=512, tf=512)

        wt = w_in[l].T.astype(BF16)
        wt_f = jnp.pad(wt[3 * fox_w:3 * fox_w + n_heads], ((0, LANES - n_heads), (0, 0)))
        qkv, f = _qkv_proj(u, wt, wt_f, n_cols=3 * fox_w, batch=batch, seq=seq, tm=1024,
                           tn=2048, q_scale=FOX_HEAD_DIM ** -0.5 * LOG2E)
        yc, ga = _conv_gate(u, wt, conv_w[l], row(conv_b[l]), gate_bias[l],
                            row0=3 * fox_w + n_heads, seq=seq, tm=1024, tn=512, sub=256)
        fb = jnp.pad(forget_bias[l], (0, LANES - n_heads)).reshape(1, LANES)
        qa, ka = _forget_aug(f, fb, batch=batch, seq=seq, n_heads=n_heads)
        y = _fox_attention(qkv, qa, ka, n_heads=n_heads, tq=256, hp=4)

        kv = _mem_kv(mem2, row(mem_kv_g[l]), w_mkv[l].astype(BF16), tm=1024)
        h = _mix_out(h, y, ga, yc, kv, w_out[l].astype(BF16), row(mix_post_g[l]),
                     row(mem_q_pre_g[l]), w_mq[l].astype(BF16), w_mo[l].astype(BF16),
                     row(mem_post_g[l]), seq=seq, tm=512, sub=256)

        h = _ffn(h, row(ffn2_pre_g[l]), ffn2_w13[l].astype(BF16), ffn2_w2[l].astype(BF16),
                 0.5 * row(ffn2_post_g[l]), tm=512, tf=512)
    return h.reshape(batch, seq, d)
```

```python
import functools
import math

import jax
import jax.numpy as jnp
from jax import lax
from jax.experimental import pallas as pl
from jax.experimental.pallas import tpu as pltpu

EPS = 1e-6
FOX_HEAD_DIM = 128
MEM_HEAD_DIM = 128
CONV_TAPS = 3

LANES = 128
SUBLANES = 8
BF16_SUBLANES = 16
MXU_WIDTH = 256
VMEM_LIMIT_BYTES = 56 * 1024 * 1024
AUG_PER_HEAD = 8
NEG_BIG = -1e30
LOG2E = math.log2(math.e)

FFN_ROWS, FFN_CHUNK = 512, 512
PROJ_ROWS = 1024
QKV_COLS, CONV_COLS = 2048, 512
MIX_ROWS = 512
ATTN_QUERY_ROWS, ATTN_HEADS_PER_STEP = MXU_WIDTH, 4

BF16 = jnp.bfloat16
F32 = jnp.float32
NT_DIMS = (((1,), (1,)), ((), ()))


def _rms(x, g):
    ms = jnp.mean(x * x, axis=-1, keepdims=True)
    return x * lax.rsqrt(ms + EPS) * g


def _params(*sem):
    return pltpu.CompilerParams(dimension_semantics=sem, vmem_limit_bytes=VMEM_LIMIT_BYTES)


def _resident(shape):
    return pl.BlockSpec(shape, lambda *_: (0,) * len(shape), pipeline_mode=pl.Buffered(1))


def _ffn_kernel(x_ref, gpre_ref, w1a_ref, w3a_ref, w2a_ref, w1b_ref, w3b_ref, w2b_ref, gpost_ref,
                *rest, emit_next, n_pairs, single_at):
    if emit_next:
        gnext_ref, o_ref, un_ref, xn_ref = rest
    else:
        o_ref, xn_ref = rest
    j = pl.program_id(1)

    def chunk(xn, w1_ref, w3_ref, w2_ref):
        gate = jnp.dot(xn, w1_ref[...], preferred_element_type=F32)
        up = jnp.dot(xn, w3_ref[...], preferred_element_type=F32)
        act = (gate * jax.nn.sigmoid(gate) * up).astype(BF16)
        return jnp.dot(act, w2_ref[...], preferred_element_type=F32)

    def pair(xn):
        return chunk(xn, w1a_ref, w3a_ref, w2a_ref) + chunk(xn, w1b_ref, w3b_ref, w2b_ref)

    @pl.when(j == 0)
    def _():
        xn = _rms(x_ref[...], gpre_ref[...]).astype(BF16)
        xn_ref[...] = xn
        o_ref[...] = pair(xn)

    @pl.when((j > 0) & (j < n_pairs) & (j != single_at))
    def _():
        o_ref[...] += pair(xn_ref[...])

    @pl.when(j == single_at)
    def _():
        o_ref[...] += chunk(xn_ref[...], w1a_ref, w3a_ref, w2a_ref)

    @pl.when(j == n_pairs)
    def _():
        half = o_ref.shape[0] // 2
        for rs in (slice(0, half), slice(half, 2 * half)):
            f = o_ref[rs, :] + pair(xn_ref[rs, :])
            h = x_ref[rs, :] + _rms(f, gpost_ref[...])
            o_ref[rs, :] = h
            if emit_next:
                un_ref[rs, :] = _rms(h, gnext_ref[...]).astype(BF16)


def _ffn(x, g_pre, w13, w2, g_post, *, g_next=None, tm, tf):
    n, d = x.shape
    d_ff = w2.shape[0]
    nf = d_ff // tf
    assert nf % 2 == 1 and nf >= 5, "d_ff chunks are walked as pairs plus one single"
    n_pairs = nf // 2
    single_at = n_pairs // 2
    chunk_a = lambda j: jnp.where(j <= single_at, 2 * j, 2 * j - 1)
    chunk_b = lambda j: jnp.where(j < single_at, 2 * j + 1,
                                  jnp.where(j == single_at, 2 * j + 2, 2 * j))
    emit_next = g_next is not None
    rows = pl.BlockSpec((tm, d), lambda i, j: (i, 0))
    vec = pl.BlockSpec((1, d), lambda i, j: (0, 0))
    args, in_specs = [x, g_pre], [rows, vec]
    for chunk in (chunk_a, chunk_b):
        args += [w13, w13, w2]
        in_specs += [
            pl.BlockSpec((d, tf), lambda i, j, c=chunk: (0, c(j))),
            pl.BlockSpec((d, tf), lambda i, j, c=chunk: (0, c(j) + nf)),
            pl.BlockSpec((tf, d), lambda i, j, c=chunk: (c(j), 0)),
        ]
    args.append(g_post); in_specs.append(vec)
    out_specs, out_shape = rows, jax.ShapeDtypeStruct((n, d), F32)
    if emit_next:
        args.append(g_next); in_specs.append(vec)
        out_specs = [out_specs, rows]
        out_shape = [out_shape, jax.ShapeDtypeStruct((n, d), BF16)]
    body = functools.partial(_ffn_kernel, emit_next=emit_next, n_pairs=n_pairs,
                             single_at=single_at)
    return pl.pallas_call(
        body,
        grid=(n // tm, n_pairs + 1),
        in_specs=in_specs,
        out_specs=out_specs,
        out_shape=out_shape,
        scratch_shapes=[pltpu.VMEM((tm, d), BF16)],
        compiler_params=_params("parallel", "arbitrary"),
        name="ffn",
    )(*args)


def _qkv_kernel(u_ref, wt_ref, wft_ref, o_ref, f_ref, *, n_scaled, scale):
    j = pl.program_id(1)
    proj = lambda wt: lax.dot_general(u_ref[...], wt, NT_DIMS, preferred_element_type=F32)

    @pl.when(j == 0)
    def _():
        f_ref[...] = proj(wft_ref[...])

    acc = proj(wt_ref[...]) * jnp.where(j < n_scaled, scale, 1.0).astype(F32)
    for hh in range(o_ref.shape[0]):
        o_ref[hh] = acc[:, hh * LANES:(hh + 1) * LANES].astype(BF16)


def _qkv_proj(u, wt, wt_f, *, n_cols, batch, seq, tm, tn, q_scale):
    n, d = u.shape
    cols = n_cols
    tiles_per_seq = seq // tm
    heads_per_blk = tn // FOX_HEAD_DIM
    body = functools.partial(_qkv_kernel, n_scaled=cols // 3 // tn, scale=q_scale)
    return pl.pallas_call(
        body,
        grid=(n // tm, cols // tn),
        in_specs=[
            pl.BlockSpec((tm, d), lambda i, j: (i, 0)),
            pl.BlockSpec((tn, d), lambda i, j: (j, 0)),
            pl.BlockSpec((LANES, d), lambda i, j: (0, 0)),
        ],
        out_specs=[
            pl.BlockSpec((None, heads_per_blk, tm, FOX_HEAD_DIM),
                         lambda i, j: (i // tiles_per_seq, j, i % tiles_per_seq, 0)),
            pl.BlockSpec((tm, LANES), lambda i, j: (i, 0)),
        ],
        out_shape=[
            jax.ShapeDtypeStruct((batch, cols // FOX_HEAD_DIM, seq, FOX_HEAD_DIM), BF16),
            jax.ShapeDtypeStruct((n, LANES), F32),
        ],
        compiler_params=_params("parallel", "arbitrary"),
        name="qkv_proj",
    )(u, wt, wt_f)


def _conv_gate_kernel(u_ref, wcx_ref, wcb_ref, wcc_ref, wga_ref, wgb_ref, cw_ref, cbias_ref,
                      gbias_ref, yc_ref, ga_ref, halo_ref, *, tiles_per_seq, sub):
    i, j = pl.program_id(0), pl.program_id(1)
    tm, tn = yc_ref.shape

    @pl.when((i == 0) & (j == 0))
    def _():
        halo_ref[...] = jnp.zeros_like(halo_ref)

    u = u_ref[...]
    top = lax.broadcasted_iota(jnp.int32, (SUBLANES, 1), 0)
    for c in range(tn // sub):
        cs = slice(c * sub, (c + 1) * sub)
        proj = lambda wt_ref: lax.dot_general(u, wt_ref[cs, :], NT_DIMS,
                                              preferred_element_type=F32)
        z = proj(wcc_ref) * proj(wcx_ref)
        zh = jnp.where(i % tiles_per_seq == 0, 0.0, halo_ref[j, :, cs])
        halo_ref[j, :, cs] = z[tm - SUBLANES:, :]

        w0, w1, w2 = cw_ref[0:1, cs], cw_ref[1:2, cs], cw_ref[2:3, cs]
        cbias = cbias_ref[:, cs]
        conv = cbias + w2 * z + w1 * pltpu.roll(z, 1, axis=0) + w0 * pltpu.roll(z, 2, axis=0)
        zt = z[:SUBLANES, :]
        conv_top = (cbias + w2 * zt
                    + w1 * jnp.where(top < 1, pltpu.roll(zh, 1, axis=0), pltpu.roll(zt, 1, axis=0))
                    + w0 * jnp.where(top < 2, pltpu.roll(zh, 2, axis=0), pltpu.roll(zt, 2, axis=0)))

        gated = jax.nn.sigmoid(proj(wgb_ref) + gbias_ref[1:2, cs]) * proj(wcb_ref)
        yc_ref[:, cs] = (gated * conv).astype(BF16)
        yc_ref[0:SUBLANES, cs] = (gated[:SUBLANES, :] * conv_top).astype(BF16)
        ga_ref[:, cs] = jax.nn.sigmoid(proj(wga_ref) + gbias_ref[0:1, cs]).astype(BF16)


def _conv_gate(u, wt, conv_w, conv_b, gate_bias, *, row0, seq, tm, tn, sub):
    n, d = u.shape
    width = conv_w.shape[1]
    nj = width // tn
    assert row0 % BF16_SUBLANES == 0 and row0 + 5 * width == wt.shape[0]
    assert width % tn == 0 and tn % sub == 0 and seq % tm == 0

    def wspec(g):
        start = lambda i, j: pl.multiple_of(row0 + (g * nj + j) * tn, BF16_SUBLANES)
        return pl.BlockSpec((pl.Element(tn), pl.Element(d)), lambda i, j: (start(i, j), 0))

    cols = lambda r: pl.BlockSpec((r, tn), lambda i, j: (0, j))
    out = pl.BlockSpec((tm, tn), lambda i, j: (i, j))
    shape = jax.ShapeDtypeStruct((n, width), BF16)
    return pl.pallas_call(
        functools.partial(_conv_gate_kernel, tiles_per_seq=seq // tm, sub=sub),
        grid=(n // tm, nj),
        in_specs=[pl.BlockSpec((tm, d), lambda i, j: (i, 0)),
                  wspec(0), wspec(1), wspec(2), wspec(3), wspec(4),
                  cols(CONV_TAPS), cols(1), cols(2)],
        out_specs=[out, out],
        out_shape=[shape, shape],
        scratch_shapes=[pltpu.VMEM((nj, SUBLANES, tn), F32)],
        compiler_params=_params("arbitrary", "arbitrary"),
        name="conv_gate",
    )(u, wt, wt, wt, wt, wt, conv_w, conv_b, gate_bias)


def _split3(x):
    hi = x.astype(BF16)
    r = x - hi.astype(F32)
    mid = r.astype(BF16)
    lo = (r - mid.astype(F32)).astype(BF16)
    return hi, mid, lo


def _forget_kernel(f_ref, b_ref, qa_ref, ka_ref, *, blk, n_heads):
    s = f_ref.shape[0]
    x = f_ref[...] + b_ref[...]
    logf = jnp.minimum(x, 0.0) - jnp.log1p(jnp.exp(-jnp.abs(x)))

    r = lax.broadcasted_iota(jnp.int32, (blk, blk), 0)
    c = lax.broadcasted_iota(jnp.int32, (blk, blk), 1)
    tri = (r >= c).astype(BF16)

    er = lax.broadcasted_iota(jnp.int32, (LANES, LANES), 0)
    ec = lax.broadcasted_iota(jnp.int32, (LANES, LANES), 1)
    lane = lax.broadcasted_iota(jnp.int32, (1, LANES), 1)
    slot = lane % AUG_PER_HEAD
    live = lane < n_heads * AUG_PER_HEAD

    def place(k):
        return ((ec == AUG_PER_HEAD * er + k) & (er < n_heads)).astype(BF16)

    spread = [jnp.concatenate([place(k), -place(k + 3)], axis=1) for k in range(3)]
    ones = jnp.concatenate([((slot >= 3) & (slot < 6) & live).astype(F32),
                            ((slot < 3) & live).astype(F32)], axis=1)

    rows = [slice(b * blk, (b + 1) * blk) for b in range(s // blk)]
    parts = [_split3(logf[rs]) for rs in rows]
    local = []
    for b in range(0, len(rows), 2):
        both = sum(jnp.dot(tri, jnp.concatenate([parts[b][p], parts[b + 1][p]], axis=1),
                           preferred_element_type=F32) for p in range(3))
        local += [both[:, :LANES], both[:, LANES:]]
    carry = jnp.zeros((1, LANES), F32)
    cums = []
    for x in local:
        cums.append(x + carry)
        carry = cums[-1][blk - 1:blk, :]
    for rs, cum in zip(rows, cums):
        c3 = _split3(cum * LOG2E)
        aug = ones + sum(jnp.dot(c3[k], spread[k], preferred_element_type=F32)
                         for k in range(3))
        qa_ref[rs, :] = aug[:, :LANES].astype(BF16)
        ka_ref[rs, :] = aug[:, LANES:].astype(BF16)


def _forget_aug(f, bias_row, *, batch, seq, n_heads):
    body = functools.partial(_forget_kernel, blk=MXU_WIDTH, n_heads=n_heads)
    spec = pl.BlockSpec((seq, LANES), lambda b: (b, 0))
    shape = jax.ShapeDtypeStruct((batch * seq, LANES), BF16)
    return pl.pallas_call(
        body,
        grid=(batch,),
        in_specs=[spec, pl.BlockSpec((1, LANES), lambda b: (0, 0))],
        out_specs=[spec, spec],
        out_shape=[shape, shape],
        compiler_params=_params("parallel"),
        name="forget_aug",
    )(f, bias_row)


def _attn_kernel(q_ref, k_ref, v_ref, qa_ref, ka_ref, o_ref, qc_ref, kc_ref, vc_ref, *, tq):
    hp, s, dh = q_ref.shape
    heads = range(hp)
    lane = lax.broadcasted_iota(jnp.int32, (1, LANES), 1)
    ones_col = jnp.broadcast_to((lane == 0).astype(BF16), (s, LANES))
    for a in heads:
        mine = (lane // AUG_PER_HEAD) == pl.program_id(1) * hp + a
        qc_ref[a, :, :dh] = q_ref[a]
        qc_ref[a, :, dh:] = qa_ref[...]
        kc_ref[a, :, :dh] = k_ref[a]
        kc_ref[a, :, dh:] = jnp.where(mine, ka_ref[...], jnp.zeros_like(ka_ref[...]))
        vc_ref[a, :, :dh] = v_ref[a]
        vc_ref[a, :, dh:] = ones_col

    row = lax.broadcasted_iota(jnp.int32, (tq, tq), 0)
    col = lax.broadcasted_iota(jnp.int32, (tq, tq), 1)
    causal = col <= row
    qk = lambda x, y: lax.dot_general(x, y, NT_DIMS, preferred_element_type=F32)
    pv = lambda x, y: jnp.dot(x.astype(BF16), y, preferred_element_type=F32)
    rowmax = lambda x: jnp.max(x, axis=-1, keepdims=True)

    for i in reversed(range(s // tq)):
        lo = i * tq
        qi = [qc_ref[a, lo:lo + tq, :] for a in heads]
        sd = [jnp.where(causal, qk(qi[a], kc_ref[a, lo:lo + tq, :]), NEG_BIG) for a in heads]
        m = [rowmax(x) for x in sd]
        if i > 0:
            so = [qk(qi[a], kc_ref[a, 0:lo, :]) for a in heads]
            m = [jnp.maximum(m[a], rowmax(so[a])) for a in heads]
        acc = [pv(jnp.exp2(sd[a] - m[a]), vc_ref[a, lo:lo + tq, :]) for a in heads]
        if i > 0:
            acc = [acc[a] + pv(jnp.exp2(so[a] - m[a]), vc_ref[a, 0:lo, :]) for a in heads]
        for a in heads:
            o_ref[a, lo:lo + tq, :] = (acc[a][:, :dh] / acc[a][:, dh:dh + 1]).astype(o_ref.dtype)


def _fox_attention(qkv, qa, ka, *, n_heads, tq, hp):
    batch, _, seq, dh = qkv.shape
    assert n_heads % hp == 0 and seq % tq == 0
    groups = n_heads // hp
    blk = lambda off: pl.BlockSpec((None, hp, seq, dh), lambda b, g: (b, off + g, 0, 0))
    aug = pl.BlockSpec((seq, LANES), lambda b, g: (b, 0))
    return pl.pallas_call(
        functools.partial(_attn_kernel, tq=tq),
        grid=(batch, groups),
        in_specs=[blk(0), blk(groups), blk(2 * groups), aug, aug],
        out_specs=blk(0),
        out_shape=jax.ShapeDtypeStruct((batch, n_heads, seq, dh), BF16),
        scratch_shapes=[pltpu.VMEM((hp, seq, dh + LANES), BF16)] * 3,
        compiler_params=_params("parallel", "arbitrary"),
        name="fox_attn",
    )(qkv, qkv, qkv, qa, ka)


def _mix_out_kernel(h_ref, y_ref, ga_ref, yc_ref, wout_ref, gmix_ref, gq_ref, wmq_ref,
                    km_ref, vm_ref, wmo_ref, gmem_ref, o_ref, *, sub):
    n_heads, tm, dh = y_ref.shape
    tiles = [slice(t * sub, (t + 1) * sub) for t in range(tm // sub)]
    mm = lambda x, w_ref: jnp.dot(x, w_ref[...], preferred_element_type=F32)

    def merge(rs):
        parts = []
        for hd in range(n_heads):
            sl = slice(hd * dh, (hd + 1) * dh)
            part = (ga_ref[rs, sl].astype(F32) * y_ref[hd, rs, :].astype(F32)
                    + yc_ref[rs, sl].astype(F32))
            parts.append(part.astype(BF16))
        return jnp.concatenate(parts, axis=-1)

    def mem_attention(qm):
        heads = []
        for hd in range(wmq_ref.shape[1] // MEM_HEAD_DIM):
            sl = slice(hd * MEM_HEAD_DIM, (hd + 1) * MEM_HEAD_DIM)
            s = lax.dot_general(qm[:, sl], km_ref[:, sl], NT_DIMS, preferred_element_type=F32)
            p = jnp.exp2(s - jnp.max(s, axis=-1, keepdims=True))
            l = jnp.sum(p, axis=-1, keepdims=True)
            o = jnp.dot(p.astype(BF16), vm_ref[:, sl], preferred_element_type=F32)
            heads.append((o / l).astype(BF16))
        return jnp.concatenate(heads, axis=-1)

    r = [mm(merge(rs), wout_ref) for rs in tiles]
    h2 = [h_ref[rs, :] + _rms(x, gmix_ref[...]) for rs, x in zip(tiles, r)]
    um = [_rms(x, gq_ref[...]).astype(BF16) for x in h2]
    qm = [(mm(x, wmq_ref) * (MEM_HEAD_DIM ** -0.5 * LOG2E)).astype(BF16) for x in um]
    om = [mem_attention(x) for x in qm]
    mo = [mm(x, wmo_ref) for x in om]
    for rs, a, b in zip(tiles, h2, mo):
        o_ref[rs, :] = a + _rms(b, gmem_ref[...])


def _mix_out(h, y, ga, yc, kv, w_out, g_mix, g_q, w_mq, w_mo, g_mem, *, seq, tm, sub):
    n, d = h.shape
    n_heads, dh = y.shape[1], y.shape[3]
    n_mem = kv.shape[0] // (n // seq)
    mw = w_mq.shape[1]
    tiles_per_seq = seq // tm
    rows = pl.BlockSpec((tm, d), lambda i: (i, 0))
    return pl.pallas_call(
        functools.partial(_mix_out_kernel, sub=sub),
        grid=(n // tm,),
        in_specs=[
            rows,
            pl.BlockSpec((None, n_heads, tm, dh),
                         lambda i: (i // tiles_per_seq, 0, i % tiles_per_seq, 0)),
            rows, rows,
            _resident((d, d)), _resident((1, d)),
            _resident((1, d)), _resident((d, mw)),
            pl.BlockSpec((n_mem, mw), lambda i: (i // tiles_per_seq, 0)),
            pl.BlockSpec((n_mem, mw), lambda i: (i // tiles_per_seq, 1)),
            _resident((mw, d)), _resident((1, d)),
        ],
        out_specs=rows,
        out_shape=jax.ShapeDtypeStruct((n, d), F32),
        compiler_params=_params("parallel"),
        name="mix_out",
    )(h, y, ga, yc, w_out, g_mix, g_q, w_mq, kv, kv, w_mo, g_mem)


def _mem_kv_kernel(x_ref, g_ref, w_ref, o_ref):
    xn = _rms(x_ref[...], g_ref[...]).astype(BF16)
    o_ref[...] = jnp.dot(xn, w_ref[...], preferred_element_type=F32).astype(BF16)


def _mem_kv(mem, g, w, *, tm):
    n, d = mem.shape
    cols = w.shape[1]
    return pl.pallas_call(
        _mem_kv_kernel,
        grid=(n // tm,),
        in_specs=[pl.BlockSpec((tm, d), lambda i: (i, 0)), _resident((1, d)),
                  _resident((d, cols))],
        out_specs=pl.BlockSpec((tm, cols), lambda i: (i, 0)),
        out_shape=jax.ShapeDtypeStruct((n, cols), BF16),
        compiler_params=_params("parallel"),
        name="mem_kv",
    )(mem, g, w)


def kernel(x, mem, ffn1_pre_g, ffn1_w13, ffn1_w2, ffn1_post_g, mix_pre_g, w_in, forget_bias,
           gate_bias, conv_w, conv_b, w_out, mix_post_g, mem_q_pre_g, mem_kv_g, w_mq, w_mkv,
           w_mo, mem_post_g, ffn2_pre_g, ffn2_w13, ffn2_w2, ffn2_post_g):
    batch, seq, d = x.shape
    n_mem = mem.shape[1]
    depth = ffn1_w13.shape[0]
    n_heads = forget_bias.shape[1]
    fox_w = n_heads * FOX_HEAD_DIM
    assert n_heads * AUG_PER_HEAD <= LANES and fox_w == d and conv_w.shape[2] == d

    row = lambda v: v.reshape(1, -1).astype(F32)
    h = x.reshape(batch * seq, d)
    mem2 = mem.reshape(batch * n_mem, d)
    for l in range(depth):
        h, u = _ffn(h, row(ffn1_pre_g[l]), ffn1_w13[l].astype(BF16), ffn1_w2[l].astype(BF16),
                    0.5 * row(ffn1_post_g[l]), g_next=row(mix_pre_g[l]),
                    tm=FFN_ROWS, tf=FFN_CHUNK)

        wt = w_in[l].T.astype(BF16)
        wt_f = jnp.pad(wt[3 * fox_w:3 * fox_w + n_heads], ((0, LANES - n_heads), (0, 0)))
        qkv, f = _qkv_proj(u, wt, wt_f, n_cols=3 * fox_w, batch=batch, seq=seq, tm=PROJ_ROWS,
                           tn=QKV_COLS, q_scale=FOX_HEAD_DIM ** -0.5 * LOG2E)
        yc, ga = _conv_gate(u, wt, conv_w[l], row(conv_b[l]), gate_bias[l],
                            row0=3 * fox_w + n_heads, seq=seq, tm=PROJ_ROWS, tn=CONV_COLS,
                            sub=MXU_WIDTH)
        fb = jnp.pad(forget_bias[l], (0, LANES - n_heads)).reshape(1, LANES)
        qa, ka = _forget_aug(f, fb, batch=batch, seq=seq, n_heads=n_heads)
        y = _fox_attention(qkv, qa, ka, n_heads=n_heads, tq=ATTN_QUERY_ROWS,
                           hp=ATTN_HEADS_PER_STEP)

        kv = _mem_kv(mem2, row(mem_kv_g[l]), w_mkv[l].astype(BF16), tm=PROJ_ROWS)
        h = _mix_out(h, y, ga, yc, kv, w_out[l].astype(BF16), row(mix_post_g[l]),
                     row(mem_q_pre_g[l]), w_mq[l].astype(BF16), w_mo[l].astype(BF16),
                     row(mem_post_g[l]), seq=seq, tm=MIX_ROWS, sub=MXU_WIDTH)

        h = _ffn(h, row(ffn2_pre_g[l]), ffn2_w13[l].astype(BF16), ffn2_w2[l].astype(BF16),
                 0.5 * row(ffn2_post_g[l]), tm=FFN_ROWS, tf=FFN_CHUNK)
    return h.reshape(batch, seq, d)
```

```python
import functools
import math

import jax
import jax.numpy as jnp
from jax import lax
from jax.experimental import pallas as pl
from jax.experimental.pallas import tpu as pltpu

EPS = 1e-6
FOX_HEAD_DIM = 128
MEM_HEAD_DIM = 128
CONV_TAPS = 3

LANES = 128
SUBLANES = 8
BF16_SUBLANES = 16
MXU_WIDTH = 256
VMEM_LIMIT_BYTES = 56 * 1024 * 1024
AUG_PER_HEAD = 8
NEG_BIG = -1e30
LOG2E = math.log2(math.e)

FFN_ROWS, FFN_CHUNK = 512, 512
PROJ_ROWS = 1024
QKV_COLS, CONV_COLS = 2048, 512
MIX_ROWS = 512
ATTN_QUERY_ROWS, ATTN_HEADS_PER_STEP = MXU_WIDTH, 4

BF16 = jnp.bfloat16
F32 = jnp.float32
NT_DIMS = (((1,), (1,)), ((), ()))


def _rms(x, g):
    ms = jnp.mean(x * x, axis=-1, keepdims=True)
    return x * lax.rsqrt(ms + EPS) * g


def _params(*sem):
    return pltpu.CompilerParams(dimension_semantics=sem, vmem_limit_bytes=VMEM_LIMIT_BYTES)


def _resident(shape):
    return pl.BlockSpec(shape, lambda *_: (0,) * len(shape), pipeline_mode=pl.Buffered(1))


def _ffn_kernel(x_ref, gpre_ref, w1a_ref, w3a_ref, w2a_ref, w1b_ref, w3b_ref, w2b_ref, gpost_ref,
                *rest, emit_next, n_pairs, single_at):
    if emit_next:
        gnext_ref, o_ref, un_ref, xn_ref = rest
    else:
        o_ref, xn_ref = rest
    j = pl.program_id(1)

    def chunk(xn, w1_ref, w3_ref, w2_ref):
        gate = jnp.dot(xn, w1_ref[...], preferred_element_type=F32)
        up = jnp.dot(xn, w3_ref[...], preferred_element_type=F32)
        act = (gate * jax.nn.sigmoid(gate) * up).astype(BF16)
        return jnp.dot(act, w2_ref[...], preferred_element_type=F32)

    def pair(xn):
        return chunk(xn, w1a_ref, w3a_ref, w2a_ref) + chunk(xn, w1b_ref, w3b_ref, w2b_ref)

    @pl.when(j == 0)
    def _():
        xn = _rms(x_ref[...], gpre_ref[...]).astype(BF16)
        xn_ref[...] = xn
        o_ref[...] = pair(xn)

    @pl.when((j > 0) & (j < n_pairs) & (j != single_at))
    def _():
        o_ref[...] += pair(xn_ref[...])

    @pl.when(j == single_at)
    def _():
        o_ref[...] += chunk(xn_ref[...], w1a_ref, w3a_ref, w2a_ref)

    @pl.when(j == n_pairs)
    def _():
        half = o_ref.shape[0] // 2
        for rs in (slice(0, half), slice(half, 2 * half)):
            f = o_ref[rs, :] + pair(xn_ref[rs, :])
            h = x_ref[rs, :] + _rms(f, gpost_ref[...])
            o_ref[rs, :] = h
            if emit_next:
                un_ref[rs, :] = _rms(h, gnext_ref[...]).astype(BF16)


def _ffn(x, g_pre, w13, w2, g_post, *, g_next=None, tm, tf):
    n, d = x.shape
    d_ff = w2.shape[0]
    nf = d_ff // tf
    assert nf % 2 == 1 and nf >= 5, "d_ff chunks are walked as pairs plus one single"
    n_pairs = nf // 2
    single_at = n_pairs // 2
    chunk_a = lambda j: jnp.where(j <= single_at, 2 * j, 2 * j - 1)
    chunk_b = lambda j: jnp.where(j < single_at, 2 * j + 1,
                                  jnp.where(j == single_at, 2 * j + 2, 2 * j))
    emit_next = g_next is not None
    rows = pl.BlockSpec((tm, d), lambda i, j: (i, 0))
    vec = pl.BlockSpec((1, d), lambda i, j: (0, 0))
    args, in_specs = [x, g_pre], [rows, vec]
    for chunk in (chunk_a, chunk_b):
        args += [w13, w13, w2]
        in_specs += [
            pl.BlockSpec((d, tf), lambda i, j, c=chunk: (0, c(j))),
            pl.BlockSpec((d, tf), lambda i, j, c=chunk: (0, c(j) + nf)),
            pl.BlockSpec((tf, d), lambda i, j, c=chunk: (c(j), 0)),
        ]
    args.append(g_post); in_specs.append(vec)
    out_specs, out_shape = rows, jax.ShapeDtypeStruct((n, d), F32)
    if emit_next:
        args.append(g_next); in_specs.append(vec)
        out_specs = [out_specs, rows]
        out_shape = [out_shape, jax.ShapeDtypeStruct((n, d), BF16)]
    body = functools.partial(_ffn_kernel, emit_next=emit_next, n_pairs=n_pairs,
                             single_at=single_at)
    return pl.pallas_call(
        body,
        grid=(n // tm, n_pairs + 1),
        in_specs=in_specs,
        out_specs=out_specs,
        out_shape=out_shape,
        scratch_shapes=[pltpu.VMEM((tm, d), BF16)],
        compiler_params=_params("parallel", "arbitrary"),
        name="ffn",
    )(*args)


def _qkv_kernel(u_ref, wt_ref, wft_ref, o_ref, f_ref, *, n_scaled, scale):
    j = pl.program_id(1)
    proj = lambda wt: lax.dot_general(u_ref[...], wt, NT_DIMS, preferred_element_type=F32)

    @pl.when(j == 0)
    def _():
        f_ref[...] = proj(wft_ref[...])

    acc = proj(wt_ref[...]) * jnp.where(j < n_scaled, scale, 1.0).astype(F32)
    for hh in range(o_ref.shape[0]):
        o_ref[hh] = acc[:, hh * LANES:(hh + 1) * LANES].astype(BF16)


def _qkv_proj(u, wt, wt_f, *, n_cols, batch, seq, tm, tn, q_scale):
    n, d = u.shape
    cols = n_cols
    tiles_per_seq = seq // tm
    heads_per_blk = tn // FOX_HEAD_DIM
    body = functools.partial(_qkv_kernel, n_scaled=cols // 3 // tn, scale=q_scale)
    return pl.pallas_call(
        body,
        grid=(n // tm, cols // tn),
        in_specs=[
            pl.BlockSpec((tm, d), lambda i, j: (i, 0)),
            pl.BlockSpec((tn, d), lambda i, j: (j, 0)),
            pl.BlockSpec((LANES, d), lambda i, j: (0, 0)),
        ],
        out_specs=[
            pl.BlockSpec((None, heads_per_blk, tm, FOX_HEAD_DIM),
                         lambda i, j: (i // tiles_per_seq, j, i % tiles_per_seq, 0)),
            pl.BlockSpec((tm, LANES), lambda i, j: (i, 0)),
        ],
        out_shape=[
            jax.ShapeDtypeStruct((batch, cols // FOX_HEAD_DIM, seq, FOX_HEAD_DIM), BF16),
            jax.ShapeDtypeStruct((n, LANES), F32),
        ],
        compiler_params=_params("parallel", "arbitrary"),
        name="qkv_proj",
    )(u, wt, wt_f)


def _conv_gate_kernel(u_ref, wcx_ref, wcb_ref, wcc_ref, wga_ref, wgb_ref, cw_ref, cbias_ref,
                      gbias_ref, yc_ref, ga_ref, halo_ref, *, tiles_per_seq, sub):
    i, j = pl.program_id(0), pl.program_id(1)
    tm, tn = yc_ref.shape

    @pl.when((i == 0) & (j == 0))
    def _():
        halo_ref[...] = jnp.zeros_like(halo_ref)

    u = u_ref[...]
    top = lax.broadcasted_iota(jnp.int32, (SUBLANES, 1), 0)
    for c in range(tn // sub):
        cs = slice(c * sub, (c + 1) * sub)
        proj = lambda wt_ref: lax.dot_general(u, wt_ref[cs, :], NT_DIMS,
                                              preferred_element_type=F32)
        z = proj(wcc_ref) * proj(wcx_ref)
        zh = jnp.where(i % tiles_per_seq == 0, 0.0, halo_ref[j, :, cs])
        halo_ref[j, :, cs] = z[tm - SUBLANES:, :]

        w0, w1, w2 = cw_ref[0:1, cs], cw_ref[1:2, cs], cw_ref[2:3, cs]
        cbias = cbias_ref[:, cs]
        conv = cbias + w2 * z + w1 * pltpu.roll(z, 1, axis=0) + w0 * pltpu.roll(z, 2, axis=0)
        zt = z[:SUBLANES, :]
        conv_top = (cbias + w2 * zt
                    + w1 * jnp.where(top < 1, pltpu.roll(zh, 1, axis=0), pltpu.roll(zt, 1, axis=0))
                    + w0 * jnp.where(top < 2, pltpu.roll(zh, 2, axis=0), pltpu.roll(zt, 2, axis=0)))

        gated = jax.nn.sigmoid(proj(wgb_ref) + gbias_ref[1:2, cs]) * proj(wcb_ref)
        yc_ref[:, cs] = (gated * conv).astype(BF16)
        yc_ref[0:SUBLANES, cs] = (gated[:SUBLANES, :] * conv_top).astype(BF16)
        ga_ref[:, cs] = jax.nn.sigmoid(proj(wga_ref) + gbias_ref[0:1, cs]).astype(BF16)


def _conv_gate(u, wt, conv_w, conv_b, gate_bias, *, row0, seq, tm, tn, sub):
    n, d = u.shape
    width = conv_w.shape[1]
    nj = width // tn
    assert row0 % BF16_SUBLANES == 0 and row0 + 5 * width == wt.shape[0]
    assert width % tn == 0 and tn % sub == 0 and seq % tm == 0

    def wspec(g):
        start = lambda i, j: pl.multiple_of(row0 + (g * nj + j) * tn, BF16_SUBLANES)
        return pl.BlockSpec((pl.Element(tn), pl.Element(d)), lambda i, j: (start(i, j), 0))

    cols = lambda r: pl.BlockSpec((r, tn), lambda i, j: (0, j))
    out = pl.BlockSpec((tm, tn), lambda i, j: (i, j))
    shape = jax.ShapeDtypeStruct((n, width), BF16)
    return pl.pallas_call(
        functools.partial(_conv_gate_kernel, tiles_per_seq=seq // tm, sub=sub),
        grid=(n // tm, nj),
        in_specs=[pl.BlockSpec((tm, d), lambda i, j: (i, 0)),
                  wspec(0), wspec(1), wspec(2), wspec(3), wspec(4),
                  cols(CONV_TAPS), cols(1), cols(2)],
        out_specs=[out, out],
        out_shape=[shape, shape],
        scratch_shapes=[pltpu.VMEM((nj, SUBLANES, tn), F32)],
        compiler_params=_params("arbitrary", "arbitrary"),
        name="conv_gate",
    )(u, wt, wt, wt, wt, wt, conv_w, conv_b, gate_bias)


def _split3(x):
    hi = x.astype(BF16)
    r = x - hi.astype(F32)
    mid = r.astype(BF16)
    lo = (r - mid.astype(F32)).astype(BF16)
    return hi, mid, lo


def _forget_kernel(f_ref, b_ref, qa_ref, ka_ref, *, blk, n_heads):
    s = f_ref.shape[0]
    x = f_ref[...] + b_ref[...]
    logf = jnp.minimum(x, 0.0) - jnp.log1p(jnp.exp(-jnp.abs(x)))

    r = lax.broadcasted_iota(jnp.int32, (blk, blk), 0)
    c = lax.broadcasted_iota(jnp.int32, (blk, blk), 1)
    tri = (r >= c).astype(BF16)

    er = lax.broadcasted_iota(jnp.int32, (LANES, LANES), 0)
    ec = lax.broadcasted_iota(jnp.int32, (LANES, LANES), 1)
    lane = lax.broadcasted_iota(jnp.int32, (1, LANES), 1)
    slot = lane % AUG_PER_HEAD
    live = lane < n_heads * AUG_PER_HEAD

    def place(k):
        return ((ec == AUG_PER_HEAD * er + k) & (er < n_heads)).astype(BF16)

    spread = [jnp.concatenate([place(k), -place(k + 3)], axis=1) for k in range(3)]
    ones = jnp.concatenate([((slot >= 3) & (slot < 6) & live).astype(F32),
                            ((slot < 3) & live).astype(F32)], axis=1)

    rows = [slice(b * blk, (b + 1) * blk) for b in range(s // blk)]
    parts = [_split3(logf[rs]) for rs in rows]
    local = []
    for b in range(0, len(rows), 2):
        both = sum(jnp.dot(tri, jnp.concatenate([parts[b][p], parts[b + 1][p]], axis=1),
                           preferred_element_type=F32) for p in range(3))
        local += [both[:, :LANES], both[:, LANES:]]
    carry = jnp.zeros((1, LANES), F32)
    cums = []
    for x in local:
        cums.append(x + carry)
        carry = cums[-1][blk - 1:blk, :]
    for rs, cum in zip(rows, cums):
        c3 = _split3(cum * LOG2E)
        aug = ones + sum(jnp.dot(c3[k], spread[k], preferred_element_type=F32)
                         for k in range(3))
        qa_ref[rs, :] = aug[:, :LANES].astype(BF16)
        ka_ref[rs, :] = aug[:, LANES:].astype(BF16)


def _forget_aug(f, bias_row, *, batch, seq, n_heads):
    body = functools.partial(_forget_kernel, blk=MXU_WIDTH, n_heads=n_heads)
    spec = pl.BlockSpec((seq, LANES), lambda b: (b, 0))
    shape = jax.ShapeDtypeStruct((batch * seq, LANES), BF16)
    return pl.pallas_call(
        body,
        grid=(batch,),
        in_specs=[spec, pl.BlockSpec((1, LANES), lambda b: (0, 0))],
        out_specs=[spec, spec],
        out_shape=[shape, shape],
        compiler_params=_params("parallel"),
        name="forget_aug",
    )(f, bias_row)


def _attn_kernel(q_ref, k_ref, v_ref, qa_ref, ka_ref, o_ref, qc_ref, kc_ref, vc_ref, *, tq):
    hp, s, dh = q_ref.shape
    heads = range(hp)
    lane = lax.broadcasted_iota(jnp.int32, (1, LANES), 1)
    ones_col = jnp.broadcast_to((lane == 0).astype(BF16), (s, LANES))
    for a in heads:
        mine = (lane // AUG_PER_HEAD) == pl.program_id(1) * hp + a
        qc_ref[a, :, :dh] = q_ref[a]
        qc_ref[a, :, dh:] = qa_ref[...]
        kc_ref[a, :, :dh] = k_ref[a]
        kc_ref[a, :, dh:] = jnp.where(mine, ka_ref[...], jnp.zeros_like(ka_ref[...]))
        vc_ref[a, :, :dh] = v_ref[a]
        vc_ref[a, :, dh:] = ones_col

    row = lax.broadcasted_iota(jnp.int32, (tq, tq), 0)
    col = lax.broadcasted_iota(jnp.int32, (tq, tq), 1)
    causal = col <= row
    qk = lambda x, y: lax.dot_general(x, y, NT_DIMS, preferred_element_type=F32)
    pv = lambda x, y: jnp.dot(x.astype(BF16), y, preferred_element_type=F32)
    rowmax = lambda x: jnp.max(x, axis=-1, keepdims=True)

    for i in reversed(range(s // tq)):
        lo = i * tq
        qi = [qc_ref[a, lo:lo + tq, :] for a in heads]
        sall = [qk(qi[a], kc_ref[a, 0:lo + tq, :]) for a in heads]
        sd = [jnp.where(causal, x[:, lo:], NEG_BIG) for x in sall]
        if i > 0:
            sall = [jnp.concatenate([x[:, :lo], y], axis=1) for x, y in zip(sall, sd)]
        else:
            sall = sd
        m = [rowmax(x) for x in sall]
        acc = [pv(jnp.exp2(sall[a] - m[a]), vc_ref[a, 0:lo + tq, :]) for a in heads]
        for a in heads:
            o_ref[a, lo:lo + tq, :] = (acc[a][:, :dh] / acc[a][:, dh:dh + 1]).astype(o_ref.dtype)


def _fox_attention(qkv, qa, ka, *, n_heads, tq, hp):
    batch, _, seq, dh = qkv.shape
    assert n_heads % hp == 0 and seq % tq == 0
    groups = n_heads // hp
    blk = lambda off: pl.BlockSpec((None, hp, seq, dh), lambda b, g: (b, off + g, 0, 0))
    aug = pl.BlockSpec((seq, LANES), lambda b, g: (b, 0))
    return pl.pallas_call(
        functools.partial(_attn_kernel, tq=tq),
        grid=(batch, groups),
        in_specs=[blk(0), blk(groups), blk(2 * groups), aug, aug],
        out_specs=blk(0),
        out_shape=jax.ShapeDtypeStruct((batch, n_heads, seq, dh), BF16),
        scratch_shapes=[pltpu.VMEM((hp, seq, dh + LANES), BF16)] * 3,
        compiler_params=_params("parallel", "arbitrary"),
        name="fox_attn",
    )(qkv, qkv, qkv, qa, ka)


def _mix_out_kernel(h_ref, y_ref, ga_ref, yc_ref, wout_ref, gmix_ref, gq_ref, wmq_ref,
                    km_ref, vm_ref, wmo_ref, gmem_ref, o_ref, *, sub):
    n_heads, tm, dh = y_ref.shape
    tiles = [slice(t * sub, (t + 1) * sub) for t in range(tm // sub)]
    mm = lambda x, w_ref: jnp.dot(x, w_ref[...], preferred_element_type=F32)

    def merge(rs):
        parts = []
        for hd in range(n_heads):
            sl = slice(hd * dh, (hd + 1) * dh)
            part = (ga_ref[rs, sl].astype(F32) * y_ref[hd, rs, :].astype(F32)
                    + yc_ref[rs, sl].astype(F32))
            parts.append(part.astype(BF16))
        return jnp.concatenate(parts, axis=-1)

    def mem_attention(qm):
        heads = []
        for hd in range(wmq_ref.shape[1] // MEM_HEAD_DIM):
            sl = slice(hd * MEM_HEAD_DIM, (hd + 1) * MEM_HEAD_DIM)
            s = lax.dot_general(qm[:, sl], km_ref[:, sl], NT_DIMS, preferred_element_type=F32)
            p = jnp.exp2(s - jnp.max(s, axis=-1, keepdims=True))
            l = jnp.sum(p, axis=-1, keepdims=True)
            o = jnp.dot(p.astype(BF16), vm_ref[:, sl], preferred_element_type=F32)
            heads.append((o / l).astype(BF16))
        return jnp.concatenate(heads, axis=-1)

    r = [mm(merge(rs), wout_ref) for rs in tiles]
    h2 = [h_ref[rs, :] + _rms(x, gmix_ref[...]) for rs, x in zip(tiles, r)]
    um = [_rms(x, gq_ref[...]).astype(BF16) for x in h2]
    qm = [(mm(x, wmq_ref) * (MEM_HEAD_DIM ** -0.5 * LOG2E)).astype(BF16) for x in um]
    om = [mem_attention(x) for x in qm]
    mo = [mm(x, wmo_ref) for x in om]
    for rs, a, b in zip(tiles, h2, mo):
        o_ref[rs, :] = a + _rms(b, gmem_ref[...])


def _mix_out(h, y, ga, yc, kv, w_out, g_mix, g_q, w_mq, w_mo, g_mem, *, seq, tm, sub):
    n, d = h.shape
    n_heads, dh = y.shape[1], y.shape[3]
    n_mem = kv.shape[0] // (n // seq)
    mw = w_mq.shape[1]
    tiles_per_seq = seq // tm
    rows = pl.BlockSpec((tm, d), lambda i: (i, 0))
    return pl.pallas_call(
        functools.partial(_mix_out_kernel, sub=sub),
        grid=(n // tm,),
        in_specs=[
            rows,
            pl.BlockSpec((None, n_heads, tm, dh),
                         lambda i: (i // tiles_per_seq, 0, i % tiles_per_seq, 0)),
            rows, rows,
            _resident((d, d)), _resident((1, d)),
            _resident((1, d)), _resident((d, mw)),
            pl.BlockSpec((n_mem, mw), lambda i: (i // tiles_per_seq, 0)),
            pl.BlockSpec((n_mem, mw), lambda i: (i // tiles_per_seq, 1)),
            _resident((mw, d)), _resident((1, d)),
        ],
        out_specs=rows,
        out_shape=jax.ShapeDtypeStruct((n, d), F32),
        compiler_params=_params("parallel"),
        name="mix_out",
    )(h, y, ga, yc, w_out, g_mix, g_q, w_mq, kv, kv, w_mo, g_mem)


def _mem_kv_kernel(x_ref, g_ref, w_ref, o_ref):
    xn = _rms(x_ref[...], g_ref[...]).astype(BF16)
    o_ref[...] = jnp.dot(xn, w_ref[...], preferred_element_type=F32).astype(BF16)


def _mem_kv(mem, g, w, *, tm):
    n, d = mem.shape
    cols = w.shape[1]
    return pl.pallas_call(
        _mem_kv_kernel,
        grid=(n // tm,),
        in_specs=[pl.BlockSpec((tm, d), lambda i: (i, 0)), _resident((1, d)),
                  _resident((d, cols))],
        out_specs=pl.BlockSpec((tm, cols), lambda i: (i, 0)),
        out_shape=jax.ShapeDtypeStruct((n, cols), BF16),
        compiler_params=_params("parallel"),
        name="mem_kv",
    )(mem, g, w)


def kernel(x, mem, ffn1_pre_g, ffn1_w13, ffn1_w2, ffn1_post_g, mix_pre_g, w_in, forget_bias,
           gate_bias, conv_w, conv_b, w_out, mix_post_g, mem_q_pre_g, mem_kv_g, w_mq, w_mkv,
           w_mo, mem_post_g, ffn2_pre_g, ffn2_w13, ffn2_w2, ffn2_post_g):
    batch, seq, d = x.shape
    n_mem = mem.shape[1]
    depth = ffn1_w13.shape[0]
    n_heads = forget_bias.shape[1]
    fox_w = n_heads * FOX_HEAD_DIM
    assert n_heads * AUG_PER_HEAD <= LANES and fox_w == d and conv_w.shape[2] == d

    row = lambda v: v.reshape(1, -1).astype(F32)
    h = x.reshape(batch * seq, d)
    mem2 = mem.reshape(batch * n_mem, d)
    for l in range(depth):
        h, u = _ffn(h, row(ffn1_pre_g[l]), ffn1_w13[l].astype(BF16), ffn1_w2[l].astype(BF16),
                    0.5 * row(ffn1_post_g[l]), g_next=row(mix_pre_g[l]),
                    tm=FFN_ROWS, tf=FFN_CHUNK)

        wt = w_in[l].T.astype(BF16)
        wt_f = jnp.pad(wt[3 * fox_w:3 * fox_w + n_heads], ((0, LANES - n_heads), (0, 0)))
        qkv, f = _qkv_proj(u, wt, wt_f, n_cols=3 * fox_w, batch=batch, seq=seq, tm=PROJ_ROWS,
                           tn=QKV_COLS, q_scale=FOX_HEAD_DIM ** -0.5 * LOG2E)
        yc, ga = _conv_gate(u, wt, conv_w[l], row(conv_b[l]), gate_bias[l],
                            row0=3 * fox_w + n_heads, seq=seq, tm=PROJ_ROWS, tn=CONV_COLS,
                            sub=MXU_WIDTH)
        fb = jnp.pad(forget_bias[l], (0, LANES - n_heads)).reshape(1, LANES)
        qa, ka = _forget_aug(f, fb, batch=batch, seq=seq, n_heads=n_heads)
        y = _fox_attention(qkv, qa, ka, n_heads=n_heads, tq=ATTN_QUERY_ROWS,
                           hp=ATTN_HEADS_PER_STEP)

        kv = _mem_kv(mem2, row(mem_kv_g[l]), w_mkv[l].astype(BF16), tm=PROJ_ROWS)
        h = _mix_out(h, y, ga, yc, kv, w_out[l].astype(BF16), row(mix_post_g[l]),
                     row(mem_q_pre_g[l]), w_mq[l].astype(BF16), w_mo[l].astype(BF16),
                     row(mem_post_g[l]), seq=seq, tm=MIX_ROWS, sub=MXU_WIDTH)

        h = _ffn(h, row(ffn2_pre_g[l]), ffn2_w13[l].astype(BF16), ffn2_w2[l].astype(BF16),
                 0.5 * row(ffn2_post_g[l]), tm=FFN_ROWS, tf=FFN_CHUNK)
    return h.reshape(batch, seq, d)
```

```python
import functools
import math

import jax
import jax.numpy as jnp
from jax import lax
from jax.experimental import pallas as pl
from jax.experimental.pallas import tpu as pltpu

EPS = 1e-6
FOX_HEAD_DIM = 128
MEM_HEAD_DIM = 128
CONV_TAPS = 3

LANES = 128
SUBLANES = 8
BF16_SUBLANES = 16
MXU_WIDTH = 256
VMEM_LIMIT_BYTES = 56 * 1024 * 1024
AUG_PER_HEAD = 8
NEG_BIG = -1e30
LOG2E = math.log2(math.e)

FFN_ROWS, FFN_CHUNK = 512, 512
PROJ_ROWS = 1024
QKV_COLS, CONV_COLS = 2048, 512
MIX_ROWS = 512
ATTN_QUERY_ROWS, ATTN_HEADS_PER_STEP = MXU_WIDTH, 4

BF16 = jnp.bfloat16
F32 = jnp.float32
NT_DIMS = (((1,), (1,)), ((), ()))


def _rms(x, g):
    ms = jnp.mean(x * x, axis=-1, keepdims=True)
    return x * lax.rsqrt(ms + EPS) * g


def _params(*sem):
    return pltpu.CompilerParams(dimension_semantics=sem, vmem_limit_bytes=VMEM_LIMIT_BYTES)


def _resident(shape):
    return pl.BlockSpec(shape, lambda *_: (0,) * len(shape), pipeline_mode=pl.Buffered(1))


def _ffn_kernel(x_ref, gpre_ref, w1a_ref, w3a_ref, w2a_ref, w1b_ref, w3b_ref, w2b_ref, gpost_ref,
                *rest, emit_next, n_pairs, single_at):
    if emit_next:
        gnext_ref, o_ref, un_ref, xn_ref = rest
    else:
        o_ref, xn_ref = rest
    j = pl.program_id(1)

    def chunk(xn, w1_ref, w3_ref, w2_ref):
        gate = jnp.dot(xn, w1_ref[...], preferred_element_type=F32)
        up = jnp.dot(xn, w3_ref[...], preferred_element_type=F32)
        act = (gate * jax.nn.sigmoid(gate) * up).astype(BF16)
        return jnp.dot(act, w2_ref[...], preferred_element_type=F32)

    def pair(xn):
        return chunk(xn, w1a_ref, w3a_ref, w2a_ref) + chunk(xn, w1b_ref, w3b_ref, w2b_ref)

    @pl.when(j == 0)
    def _():
        xn = _rms(x_ref[...], gpre_ref[...]).astype(BF16)
        xn_ref[...] = xn
        o_ref[...] = pair(xn)

    @pl.when((j > 0) & (j < n_pairs) & (j != single_at))
    def _():
        o_ref[...] += pair(xn_ref[...])

    @pl.when(j == single_at)
    def _():
        o_ref[...] += chunk(xn_ref[...], w1a_ref, w3a_ref, w2a_ref)

    @pl.when(j == n_pairs)
    def _():
        half = o_ref.shape[0] // 2
        for rs in (slice(0, half), slice(half, 2 * half)):
            f = o_ref[rs, :] + pair(xn_ref[rs, :])
            h = x_ref[rs, :] + _rms(f, gpost_ref[...])
            o_ref[rs, :] = h
            if emit_next:
                un_ref[rs, :] = _rms(h, gnext_ref[...]).astype(BF16)


def _ffn(x, g_pre, w13, w2, g_post, *, g_next=None, tm, tf):
    n, d = x.shape
    d_ff = w2.shape[0]
    nf = d_ff // tf
    assert nf % 2 == 1 and nf >= 5, "d_ff chunks are walked as pairs plus one single"
    n_pairs = nf // 2
    single_at = n_pairs // 2
    chunk_a = lambda j: jnp.where(j <= single_at, 2 * j, 2 * j - 1)
    chunk_b = lambda j: jnp.where(j < single_at, 2 * j + 1,
                                  jnp.where(j == single_at, 2 * j + 2, 2 * j))
    emit_next = g_next is not None
    rows = pl.BlockSpec((tm, d), lambda i, j: (i, 0))
    vec = pl.BlockSpec((1, d), lambda i, j: (0, 0))
    args, in_specs = [x, g_pre], [rows, vec]
    for chunk in (chunk_a, chunk_b):
        args += [w13, w13, w2]
        in_specs += [
            pl.BlockSpec((d, tf), lambda i, j, c=chunk: (0, c(j))),
            pl.BlockSpec((d, tf), lambda i, j, c=chunk: (0, c(j) + nf)),
            pl.BlockSpec((tf, d), lambda i, j, c=chunk: (c(j), 0)),
        ]
    args.append(g_post); in_specs.append(vec)
    out_specs, out_shape = rows, jax.ShapeDtypeStruct((n, d), F32)
    if emit_next:
        args.append(g_next); in_specs.append(vec)
        out_specs = [out_specs, rows]
        out_shape = [out_shape, jax.ShapeDtypeStruct((n, d), BF16)]
    body = functools.partial(_ffn_kernel, emit_next=emit_next, n_pairs=n_pairs,
                             single_at=single_at)
    return pl.pallas_call(
        body,
        grid=(n // tm, n_pairs + 1),
        in_specs=in_specs,
        out_specs=out_specs,
        out_shape=out_shape,
        scratch_shapes=[pltpu.VMEM((tm, d), BF16)],
        compiler_params=_params("parallel", "arbitrary"),
        name="ffn",
    )(*args)


def _qkv_kernel(u_ref, wt_ref, wft_ref, o_ref, f_ref, *, n_scaled, scale):
    j = pl.program_id(1)
    proj = lambda wt: lax.dot_general(u_ref[...], wt, NT_DIMS, preferred_element_type=F32)

    @pl.when(j == 0)
    def _():
        f_ref[...] = proj(wft_ref[...])

    acc = proj(wt_ref[...]) * jnp.where(j < n_scaled, scale, 1.0).astype(F32)
    for hh in range(o_ref.shape[0]):
        o_ref[hh] = acc[:, hh * LANES:(hh + 1) * LANES].astype(BF16)


def _qkv_proj(u, wt, wt_f, *, n_cols, batch, seq, tm, tn, q_scale):
    n, d = u.shape
    cols = n_cols
    tiles_per_seq = seq // tm
    heads_per_blk = tn // FOX_HEAD_DIM
    body = functools.partial(_qkv_kernel, n_scaled=cols // 3 // tn, scale=q_scale)
    return pl.pallas_call(
        body,
        grid=(n // tm, cols // tn),
        in_specs=[
            pl.BlockSpec((tm, d), lambda i, j: (i, 0)),
            pl.BlockSpec((tn, d), lambda i, j: (j, 0)),
            pl.BlockSpec((LANES, d), lambda i, j: (0, 0)),
        ],
        out_specs=[
            pl.BlockSpec((None, heads_per_blk, tm, FOX_HEAD_DIM),
                         lambda i, j: (i // tiles_per_seq, j, i % tiles_per_seq, 0)),
            pl.BlockSpec((tm, LANES), lambda i, j: (i, 0)),
        ],
        out_shape=[
            jax.ShapeDtypeStruct((batch, cols // FOX_HEAD_DIM, seq, FOX_HEAD_DIM), BF16),
            jax.ShapeDtypeStruct((n, LANES), F32),
        ],
        compiler_params=_params("parallel", "arbitrary"),
        name="qkv_proj",
    )(u, wt, wt_f)


def _conv_gate_kernel(u_ref, wcx_ref, wcb_ref, wcc_ref, wga_ref, wgb_ref, cw_ref, cbias_ref,
                      gbias_ref, yc_ref, ga_ref, halo_ref, *, tiles_per_seq, sub):
    i, j = pl.program_id(0), pl.program_id(1)
    tm, tn = yc_ref.shape

    @pl.when((i == 0) & (j == 0))
    def _():
        halo_ref[...] = jnp.zeros_like(halo_ref)

    u = u_ref[...]
    top = lax.broadcasted_iota(jnp.int32, (SUBLANES, 1), 0)
    for c in range(tn // sub):
        cs = slice(c * sub, (c + 1) * sub)
        proj = lambda wt_ref: lax.dot_general(u, wt_ref[cs, :], NT_DIMS,
                                              preferred_element_type=F32)
        z = proj(wcc_ref) * proj(wcx_ref)
        zh = jnp.where(i % tiles_per_seq == 0, 0.0, halo_ref[j, :, cs])
        halo_ref[j, :, cs] = z[tm - SUBLANES:, :]

        w0, w1, w2 = cw_ref[0:1, cs], cw_ref[1:2, cs], cw_ref[2:3, cs]
        cbias = cbias_ref[:, cs]
        conv = cbias + w2 * z + w1 * pltpu.roll(z, 1, axis=0) + w0 * pltpu.roll(z, 2, axis=0)
        zt = z[:SUBLANES, :]
        conv_top = (cbias + w2 * zt
                    + w1 * jnp.where(top < 1, pltpu.roll(zh, 1, axis=0), pltpu.roll(zt, 1, axis=0))
                    + w0 * jnp.where(top < 2, pltpu.roll(zh, 2, axis=0), pltpu.roll(zt, 2, axis=0)))

        gated = jax.nn.sigmoid(proj(wgb_ref) + gbias_ref[1:2, cs]) * proj(wcb_ref)
        yc_ref[:, cs] = (gated * conv).astype(BF16)
        yc_ref[0:SUBLANES, cs] = (gated[:SUBLANES, :] * conv_top).astype(BF16)
        ga_ref[:, cs] = jax.nn.sigmoid(proj(wga_ref) + gbias_ref[0:1, cs]).astype(BF16)


def _conv_gate(u, wt, conv_w, conv_b, gate_bias, *, row0, seq, tm, tn, sub):
    n, d = u.shape
    width = conv_w.shape[1]
    nj = width // tn
    assert row0 % BF16_SUBLANES == 0 and row0 + 5 * width == wt.shape[0]
    assert width % tn == 0 and tn % sub == 0 and seq % tm == 0

    def wspec(g):
        start = lambda i, j: pl.multiple_of(row0 + (g * nj + j) * tn, BF16_SUBLANES)
        return pl.BlockSpec((pl.Element(tn), pl.Element(d)), lambda i, j: (start(i, j), 0))

    cols = lambda r: pl.BlockSpec((r, tn), lambda i, j: (0, j))
    out = pl.BlockSpec((tm, tn), lambda i, j: (i, j))
    shape = jax.ShapeDtypeStruct((n, width), BF16)
    return pl.pallas_call(
        functools.partial(_conv_gate_kernel, tiles_per_seq=seq // tm, sub=sub),
        grid=(n // tm, nj),
        in_specs=[pl.BlockSpec((tm, d), lambda i, j: (i, 0)),
                  wspec(0), wspec(1), wspec(2), wspec(3), wspec(4),
                  cols(CONV_TAPS), cols(1), cols(2)],
        out_specs=[out, out],
        out_shape=[shape, shape],
        scratch_shapes=[pltpu.VMEM((nj, SUBLANES, tn), F32)],
        compiler_params=_params("arbitrary", "arbitrary"),
        name="conv_gate",
    )(u, wt, wt, wt, wt, wt, conv_w, conv_b, gate_bias)


def _split3(x):
    hi = x.astype(BF16)
    r = x - hi.astype(F32)
    mid = r.astype(BF16)
    lo = (r - mid.astype(F32)).astype(BF16)
    return hi, mid, lo


def _forget_kernel(f_ref, b_ref, qa_ref, ka_ref, *, blk, n_heads):
    s = f_ref.shape[0]
    x = f_ref[...] + b_ref[...]
    logf = jnp.minimum(x, 0.0) - jnp.log1p(jnp.exp(-jnp.abs(x)))

    r = lax.broadcasted_iota(jnp.int32, (blk, blk), 0)
    c = lax.broadcasted_iota(jnp.int32, (blk, blk), 1)
    tri = (r >= c).astype(BF16)

    er = lax.broadcasted_iota(jnp.int32, (LANES, LANES), 0)
    ec = lax.broadcasted_iota(jnp.int32, (LANES, LANES), 1)
    lane = lax.broadcasted_iota(jnp.int32, (1, LANES), 1)
    slot = lane % AUG_PER_HEAD
    live = lane < n_heads * AUG_PER_HEAD

    def place(k):
        return ((ec == AUG_PER_HEAD * er + k) & (er < n_heads)).astype(BF16)

    spread = [jnp.concatenate([place(k), -place(k + 3)], axis=1) for k in range(3)]
    ones = jnp.concatenate([((slot >= 3) & (slot < 6) & live).astype(F32),
                            ((slot < 3) & live).astype(F32)], axis=1)

    rows = [slice(b * blk, (b + 1) * blk) for b in range(s // blk)]
    parts = [_split3(logf[rs]) for rs in rows]
    local = []
    for b in range(0, len(rows), 2):
        both = sum(jnp.dot(tri, jnp.concatenate([parts[b][p], parts[b + 1][p]], axis=1),
                           preferred_element_type=F32) for p in range(3))
        local += [both[:, :LANES], both[:, LANES:]]
    carry = jnp.zeros((1, LANES), F32)
    cums = []
    for x in local:
        cums.append(x + carry)
        carry = cums[-1][blk - 1:blk, :]
    for rs, cum in zip(rows, cums):
        c3 = _split3(cum * LOG2E)
        aug = ones + sum(jnp.dot(c3[k], spread[k], preferred_element_type=F32)
                         for k in range(3))
        qa_ref[rs, :] = aug[:, :LANES].astype(BF16)
        ka_ref[rs, :] = aug[:, LANES:].astype(BF16)


def _forget_aug(f, bias_row, *, batch, seq, n_heads):
    body = functools.partial(_forget_kernel, blk=MXU_WIDTH, n_heads=n_heads)
    spec = pl.BlockSpec((seq, LANES), lambda b: (b, 0))
    shape = jax.ShapeDtypeStruct((batch * seq, LANES), BF16)
    return pl.pallas_call(
        body,
        grid=(batch,),
        in_specs=[spec, pl.BlockSpec((1, LANES), lambda b: (0, 0))],
        out_specs=[spec, spec],
        out_shape=[shape, shape],
        compiler_params=_params("parallel"),
        name="forget_aug",
    )(f, bias_row)


def _attn_kernel(q_ref, k_ref, v_ref, qa_ref, ka_ref, o_ref, qc_ref, kc_ref, vc_ref, *, tq):
    hp, s, dh = q_ref.shape
    heads = range(hp)
    lane = lax.broadcasted_iota(jnp.int32, (1, LANES), 1)
    ones_col = jnp.broadcast_to((lane == 0).astype(BF16), (s, LANES))
    for a in heads:
        mine = (lane // AUG_PER_HEAD) == pl.program_id(1) * hp + a
        qc_ref[a, :, :dh] = q_ref[a]
        qc_ref[a, :, dh:] = qa_ref[...]
        kc_ref[a, :, :dh] = k_ref[a]
        kc_ref[a, :, dh:] = jnp.where(mine, ka_ref[...], jnp.zeros_like(ka_ref[...]))
        vc_ref[a, :, :dh] = v_ref[a]
        vc_ref[a, :, dh:] = ones_col

    row = lax.broadcasted_iota(jnp.int32, (tq, tq), 0)
    col = lax.broadcasted_iota(jnp.int32, (tq, tq), 1)
    causal = col <= row
    qk = lambda x, y: lax.dot_general(x, y, NT_DIMS, preferred_element_type=F32)
    pv = lambda x, y: jnp.dot(x.astype(BF16), y, preferred_element_type=F32)
    rowmax = lambda x: jnp.max(x, axis=-1, keepdims=True)

    for i in reversed(range(s // tq)):
        lo = i * tq
        qi = [qc_ref[a, lo:lo + tq, :] for a in heads]
        sall = [qk(qi[a], kc_ref[a, 0:lo + tq, :]) for a in heads]
        sd = [jnp.where(causal, x[:, lo:], NEG_BIG) for x in sall]
        if i > 0:
            sall = [jnp.concatenate([x[:, :lo], y], axis=1) for x, y in zip(sall, sd)]
        else:
            sall = sd
        m = [rowmax(x) for x in sall]
        acc = [pv(jnp.exp2(sall[a] - m[a]), vc_ref[a, 0:lo + tq, :]) for a in heads]
        for a in heads:
            o_ref[a, lo:lo + tq, :] = (acc[a][:, :dh] / acc[a][:, dh:dh + 1]).astype(o_ref.dtype)


def _fox_attention(qkv, qa, ka, *, n_heads, tq, hp):
    batch, _, seq, dh = qkv.shape
    assert n_heads % hp == 0 and seq % tq == 0
    groups = n_heads // hp
    blk = lambda off: pl.BlockSpec((None, hp, seq, dh), lambda b, g: (b, off + g, 0, 0))
    aug = pl.BlockSpec((seq, LANES), lambda b, g: (b, 0))
    return pl.pallas_call(
        functools.partial(_attn_kernel, tq=tq),
        grid=(batch, groups),
        in_specs=[blk(0), blk(groups), blk(2 * groups), aug, aug],
        out_specs=blk(0),
        out_shape=jax.ShapeDtypeStruct((batch, n_heads, seq, dh), BF16),
        scratch_shapes=[pltpu.VMEM((hp, seq, dh + LANES), BF16)] * 3,
        compiler_params=_params("parallel", "arbitrary"),
        name="fox_attn",
    )(qkv, qkv, qkv, qa, ka)


def _mix_out_kernel(h_ref, y_ref, ga_ref, yc_ref, wout_ref, gmix_ref, gq_ref, wmq_ref,
                    km_ref, vm_ref, wmo_ref, gmem_ref, o_ref, *, sub):
    n_heads, tm, dh = y_ref.shape
    tiles = [slice(t * sub, (t + 1) * sub) for t in range(tm // sub)]
    mm = lambda x, w_ref: jnp.dot(x, w_ref[...], preferred_element_type=F32)

    def merge(rs):
        parts = []
        for hd in range(n_heads):
            sl = slice(hd * dh, (hd + 1) * dh)
            parts.append(ga_ref[rs, sl] * y_ref[hd, rs, :] + yc_ref[rs, sl])
        return jnp.concatenate(parts, axis=-1)

    def mem_attention(qm):
        heads = []
        for hd in range(wmq_ref.shape[1] // MEM_HEAD_DIM):
            sl = slice(hd * MEM_HEAD_DIM, (hd + 1) * MEM_HEAD_DIM)
            s = lax.dot_general(qm[:, sl], km_ref[:, sl], NT_DIMS, preferred_element_type=F32)
            p = jnp.exp2(s - jnp.max(s, axis=-1, keepdims=True))
            l = jnp.sum(p, axis=-1, keepdims=True)
            o = jnp.dot(p.astype(BF16), vm_ref[:, sl], preferred_element_type=F32)
            heads.append((o / l).astype(BF16))
        return jnp.concatenate(heads, axis=-1)

    r = [mm(merge(rs), wout_ref) for rs in tiles]
    h2 = [h_ref[rs, :] + _rms(x, gmix_ref[...]) for rs, x in zip(tiles, r)]
    um = [_rms(x, gq_ref[...]).astype(BF16) for x in h2]
    qm = [(mm(x, wmq_ref) * (MEM_HEAD_DIM ** -0.5 * LOG2E)).astype(BF16) for x in um]
    om = [mem_attention(x) for x in qm]
    mo = [mm(x, wmo_ref) for x in om]
    for rs, a, b in zip(tiles, h2, mo):
        o_ref[rs, :] = a + _rms(b, gmem_ref[...])


def _mix_out(h, y, ga, yc, kv, w_out, g_mix, g_q, w_mq, w_mo, g_mem, *, seq, tm, sub):
    n, d = h.shape
    n_heads, dh = y.shape[1], y.shape[3]
    n_mem = kv.shape[0] // (n // seq)
    mw = w_mq.shape[1]
    tiles_per_seq = seq // tm
    rows = pl.BlockSpec((tm, d), lambda i: (i, 0))
    return pl.pallas_call(
        functools.partial(_mix_out_kernel, sub=sub),
        grid=(n // tm,),
        in_specs=[
            rows,
            pl.BlockSpec((None, n_heads, tm, dh),
                         lambda i: (i // tiles_per_seq, 0, i % tiles_per_seq, 0)),
            rows, rows,
            _resident((d, d)), _resident((1, d)),
            _resident((1, d)), _resident((d, mw)),
            pl.BlockSpec((n_mem, mw), lambda i: (i // tiles_per_seq, 0)),
            pl.BlockSpec((n_mem, mw), lambda i: (i // tiles_per_seq, 1)),
            _resident((mw, d)), _resident((1, d)),
        ],
        out_specs=rows,
        out_shape=jax.ShapeDtypeStruct((n, d), F32),
        compiler_params=_params("parallel"),
        name="mix_out",
    )(h, y, ga, yc, w_out, g_mix, g_q, w_mq, kv, kv, w_mo, g_mem)


def _mem_kv_kernel(x_ref, g_ref, w_ref, o_ref):
    xn = _rms(x_ref[...], g_ref[...]).astype(BF16)
    o_ref[...] = jnp.dot(xn, w_ref[...], preferred_element_type=F32).astype(BF16)


def _mem_kv(mem, g, w, *, tm):
    n, d = mem.shape
    cols = w.shape[1]
    return pl.pallas_call(
        _mem_kv_kernel,
        grid=(n // tm,),
        in_specs=[pl.BlockSpec((tm, d), lambda i: (i, 0)), _resident((1, d)),
                  _resident((d, cols))],
        out_specs=pl.BlockSpec((tm, cols), lambda i: (i, 0)),
        out_shape=jax.ShapeDtypeStruct((n, cols), BF16),
        compiler_params=_params("parallel"),
        name="mem_kv",
    )(mem, g, w)


def kernel(x, mem, ffn1_pre_g, ffn1_w13, ffn1_w2, ffn1_post_g, mix_pre_g, w_in, forget_bias,
           gate_bias, conv_w, conv_b, w_out, mix_post_g, mem_q_pre_g, mem_kv_g, w_mq, w_mkv,
           w_mo, mem_post_g, ffn2_pre_g, ffn2_w13, ffn2_w2, ffn2_post_g):
    batch, seq, d = x.shape
    n_mem = mem.shape[1]
    depth = ffn1_w13.shape[0]
    n_heads = forget_bias.shape[1]
    fox_w = n_heads * FOX_HEAD_DIM
    assert n_heads * AUG_PER_HEAD <= LANES and fox_w == d and conv_w.shape[2] == d

    row = lambda v: v.reshape(1, -1).astype(F32)
    h = x.reshape(batch * seq, d)
    mem2 = mem.reshape(batch * n_mem, d)
    for l in range(depth):
        h, u = _ffn(h, row(ffn1_pre_g[l]), ffn1_w13[l].astype(BF16), ffn1_w2[l].astype(BF16),
                    0.5 * row(ffn1_post_g[l]), g_next=row(mix_pre_g[l]),
                    tm=FFN_ROWS, tf=FFN_CHUNK)

        wt = w_in[l].T.astype(BF16)
        wt_f = jnp.pad(wt[3 * fox_w:3 * fox_w + n_heads], ((0, LANES - n_heads), (0, 0)))
        qkv, f = _qkv_proj(u, wt, wt_f, n_cols=3 * fox_w, batch=batch, seq=seq, tm=PROJ_ROWS,
                           tn=QKV_COLS, q_scale=FOX_HEAD_DIM ** -0.5 * LOG2E)
        yc, ga = _conv_gate(u, wt, conv_w[l], row(conv_b[l]), gate_bias[l],
                            row0=3 * fox_w + n_heads, seq=seq, tm=PROJ_ROWS, tn=CONV_COLS,
                            sub=MXU_WIDTH)
        fb = jnp.pad(forget_bias[l], (0, LANES - n_heads)).reshape(1, LANES)
        qa, ka = _forget_aug(f, fb, batch=batch, seq=seq, n_heads=n_heads)
        y = _fox_attention(qkv, qa, ka, n_heads=n_heads, tq=ATTN_QUERY_ROWS,
                           hp=ATTN_HEADS_PER_STEP)

        kv = _mem_kv(mem2, row(mem_kv_g[l]), w_mkv[l].astype(BF16), tm=PROJ_ROWS)
        h = _mix_out(h, y, ga, yc, kv, w_out[l].astype(BF16), row(mix_post_g[l]),
                     row(mem_q_pre_g[l]), w_mq[l].astype(BF16), w_mo[l].astype(BF16),
                     row(mem_post_g[l]), seq=seq, tm=MIX_ROWS, sub=MXU_WIDTH)

        h = _ffn(h, row(ffn2_pre_g[l]), ffn2_w13[l].astype(BF16), ffn2_w2[l].astype(BF16),
                 0.5 * row(ffn2_post_g[l]), tm=FFN_ROWS, tf=FFN_CHUNK)
    return h.reshape(batch, seq, d)
```
